```python
import math
import jax, jax.numpy as jnp
from jax import lax
import numpy as np


D_MODEL = 1024
BATCH = 8
SEQ = 4096
DEPTH = 1

D_SSM = 512
SSM_GROUP = 16
N_SSM_GROUPS = D_SSM // SSM_GROUP
SSM_STATE = 64
D_ATTN = 512
HEAD_DIM = 64
N_HEADS = D_ATTN // HEAD_DIM
D_MIX = D_SSM + D_ATTN
D_IN = D_SSM + 3 * D_ATTN
MOBA_BLOCK = 256
MOBA_TOPK = 3
Q_CHUNK = 32
N_BUCKETS = 32
MAX_DISTANCE = 128
N_MEM = 256
X_HEADS = 4
X_HEAD_DIM = 128
D_X = X_HEADS * X_HEAD_DIM
D_FF = 2816
CONV_W = 3
EPS = 1e-6
NEG = -1e30

kernel_name = 'hybrid_s5_moba_memxattn_convffn'


def rms_norm(x, g):
    xf = x.astype(jnp.float32)
    y = xf * lax.rsqrt(jnp.mean(xf * xf, axis=-1, keepdims=True) + EPS)
    return (y * g.astype(jnp.float32)).astype(x.dtype)


def t5_bucket(dist):
    n = jnp.maximum(dist, 0)
    max_exact = N_BUCKETS // 2
    nf = jnp.maximum(n, max_exact).astype(jnp.float32)
    large = max_exact + (jnp.log(nf / max_exact) / math.log(MAX_DISTANCE / max_exact)
                         * (N_BUCKETS - max_exact)).astype(jnp.int32)
    large = jnp.minimum(large, N_BUCKETS - 1)
    return jnp.where(n < max_exact, n, large)


def s5_mixer(u, lam_re, lam_im, log_step, b_re, b_im, c_re, c_im, d_skip, w_glu, b_glu):
    bsz, t = u.shape[0], u.shape[1]
    f32 = jnp.float32
    dt_ = u.dtype
    a_re = jnp.minimum(lam_re.astype(f32), -1e-4)
    a_im = lam_im.astype(f32)
    step = jnp.exp(log_step.astype(f32))[:, None]
    mag = jnp.exp(step * a_re)
    ab_re = mag * jnp.cos(step * a_im)
    ab_im = mag * jnp.sin(step * a_im)
    den = a_re * a_re + a_im * a_im
    p = ab_re - 1.0
    f_re = (p * a_re + ab_im * a_im) / den
    f_im = (ab_im * a_re - p * a_im) / den
    bb_re = f_re[..., None] * b_re - f_im[..., None] * b_im
    bb_im = f_re[..., None] * b_im + f_im[..., None] * b_re
    ug = u.reshape(bsz, t, N_SSM_GROUPS, SSM_GROUP)
    bu_re = jnp.einsum('btgc,gnc->btgn', ug, bb_re.astype(dt_))
    bu_im = jnp.einsum('btgc,gnc->btgn', ug, bb_im.astype(dt_))
    a_re_t = jnp.broadcast_to(ab_re.astype(dt_), (1, t) + ab_re.shape)
    a_im_t = jnp.broadcast_to(ab_im.astype(dt_), (1, t) + ab_im.shape)

    def combine(e1, e2):
        a1r, a1i, b1r, b1i = e1
        a2r, a2i, b2r, b2i = e2
        return (a2r * a1r - a2i * a1i,
                a2r * a1i + a2i * a1r,
                a2r * b1r - a2i * b1i + b2r,
                a2r * b1i + a2i * b1r + b2i)

    _, _, s_re, s_im = lax.associative_scan(combine, (a_re_t, a_im_t, bu_re, bu_im), axis=1)
    y = (jnp.einsum('gcn,btgn->btgc', c_re.astype(dt_), s_re)
         - jnp.einsum('gcn,btgn->btgc', c_im.astype(dt_), s_im))
    y = y.reshape(bsz, t, D_SSM) + d_skip * u
    y = jax.nn.gelu(y)
    return y * jax.nn.sigmoid(y @ w_glu + b_glu)


def moba_attention(q, k, v, rel_bias):
    bsz, t = q.shape[0], q.shape[1]
    f32 = jnp.float32
    nb = -(-t // MOBA_BLOCK)
    t_pad = nb * MOBA_BLOCK
    pad = ((0, 0), (0, t_pad - t), (0, 0), (0, 0))
    qp = jnp.pad(q, pad).transpose(0, 2, 1, 3)
    kb = jnp.pad(k, pad).transpose(0, 2, 1, 3).reshape(bsz, N_HEADS, nb, MOBA_BLOCK, HEAD_DIM)
    vb = jnp.pad(v, pad).transpose(0, 2, 1, 3).reshape(bsz, N_HEADS, nb, MOBA_BLOCK, HEAD_DIM)
    k_mean = jnp.mean(kb.astype(f32), axis=3).astype(q.dtype)
    top_k = min(MOBA_TOPK, nb)
    bias_t = rel_bias.T
    scale = HEAD_DIM ** -0.5
    b_ix = jnp.arange(bsz)[:, None, None, None]
    h_ix = jnp.arange(N_HEADS)[None, :, None, None]
    h_ix5 = h_ix[..., None]
    blk_ar = jnp.arange(MOBA_BLOCK, dtype=jnp.int32)
    blocks = jnp.arange(nb, dtype=jnp.int32)

    def chunk(c):
        q0 = c * Q_CHUNK
        own = q0 // MOBA_BLOCK
        qc = lax.dynamic_slice_in_dim(qp, q0, Q_CHUNK, axis=2)
        qpos = q0 + jnp.arange(Q_CHUNK, dtype=jnp.int32)
        gate = jnp.einsum('bhqd,bhnd->bhqn', qc, k_mean).astype(f32)
        gate = jnp.where(blocks < own, gate, -jnp.inf)
        gval, gidx = lax.top_k(gate, top_k)
        sel_ok = jnp.isfinite(gval)
        ks = kb[b_ix, h_ix, gidx]
        vs = vb[b_ix, h_ix, gidx]
        kpos_sel = gidx[..., None] * MOBA_BLOCK + blk_ar
        s_sel = jnp.einsum('bhqd,bhqskd->bhqsk', qc, ks).astype(f32) * scale
        s_sel = s_sel + bias_t[h_ix5, t5_bucket(qpos[:, None, None] - kpos_sel)].astype(f32)
        s_sel = jnp.where(sel_ok[..., None], s_sel, NEG)
        k_own = lax.dynamic_index_in_dim(kb, own, axis=2, keepdims=False)
        v_own = lax.dynamic_index_in_dim(vb, own, axis=2, keepdims=False)
        dist_own = qpos[:, None] - (own * MOBA_BLOCK + blk_ar)[None, :]
        s_own = jnp.einsum('bhqd,bhkd->bhqk', qc, k_own).astype(f32) * scale
        s_own = s_own + bias_t[:, t5_bucket(dist_own)].astype(f32)
        s_own = jnp.where(dist_own >= 0, s_own, NEG)
        logits = jnp.concatenate(
            [s_sel.reshape(bsz, N_HEADS, Q_CHUNK, top_k * MOBA_BLOCK), s_own], axis=-1)
        prob = jax.nn.softmax(logits, axis=-1).astype(v.dtype)
        p_sel = prob[..., :top_k * MOBA_BLOCK].reshape(bsz, N_HEADS, Q_CHUNK, top_k, MOBA_BLOCK)
        p_own = prob[..., top_k * MOBA_BLOCK:]
        return (jnp.einsum('bhqsk,bhqskd->bhqd', p_sel, vs)
                + jnp.einsum('bhqk,bhkd->bhqd', p_own, v_own))

    outs = lax.map(chunk, jnp.arange(t_pad // Q_CHUNK, dtype=jnp.int32))
    out = outs.transpose(1, 0, 3, 2, 4).reshape(bsz, t_pad, D_ATTN)
    return out[:, :t]


def memory_cross_attention(h, mem_n, w_q, w_kv, w_o, qn, kn):
    bsz, t = h.shape[0], h.shape[1]
    m = mem_n.shape[1]
    q = rms_norm((h @ w_q).reshape(bsz, t, X_HEADS, X_HEAD_DIM), qn)
    kv = mem_n @ w_kv
    k = rms_norm(kv[..., :D_X].reshape(bsz, m, X_HEADS, X_HEAD_DIM), kn)
    v = kv[..., D_X:].reshape(bsz, m, X_HEADS, X_HEAD_DIM)
    s = jnp.einsum('bqhd,bmhd->bhqm', q, k).astype(jnp.float32) * (X_HEAD_DIM ** -0.5)
    prob = jax.nn.softmax(s, axis=-1).astype(v.dtype)
    o = jnp.einsum('bhqm,bmhd->bqhd', prob, v).reshape(bsz, t, D_X)
    return o @ w_o


def conv_ffn(h, w_up, conv_w, conv_b, w_down):
    up = h @ w_up
    upc = lax.conv_general_dilated(
        up, conv_w.astype(up.dtype), window_strides=(1,), padding=[(CONV_W - 1, 0)],
        dimension_numbers=('NWC', 'WIO', 'NWC'), feature_group_count=2 * D_FF) + conv_b
    gate, val = upc[..., :D_FF], upc[..., D_FF:]
    return (jax.nn.silu(gate) * val) @ w_down


def setup_inputs(seed: int = 0) -> dict:
    key = jax.random.key(seed)
    ks = jax.random.split(key, 40)
    f32 = jnp.float32
    L = DEPTH

    def nrm(k, shape, scale):
        return jax.random.normal(k, shape, f32) * scale

    def gain(k, shape):
        return 1.0 + 0.02 * jax.random.normal(k, shape, f32)

    G, N, C = N_SSM_GROUPS, SSM_STATE, SSM_GROUP
    return {
        'x': nrm(ks[0], (BATCH, SEQ, D_MODEL), 1.0),
        'mem': nrm(ks[1], (BATCH, N_MEM, D_MODEL), 1.0),
        'norm_mix': gain(ks[2], (L, D_MODEL)),
        'w_in': nrm(ks[3], (L, D_MODEL, D_IN), D_MODEL ** -0.5),
        'attn_qn': gain(ks[4], (L, HEAD_DIM)),
        'attn_kn': gain(ks[5], (L, HEAD_DIM)),
        'ssm_lam_re': -0.5 + 0.01 * jax.random.normal(ks[6], (L, G, N), f32),
        'ssm_lam_im': jnp.pi * jnp.arange(N, dtype=f32) + 0.01 * jax.random.normal(ks[7], (L, G, N), f32),
        'ssm_log_step': jax.random.uniform(ks[8], (L, G), f32, minval=math.log(1e-3), maxval=math.log(1e-1)),
        'ssm_b_re': nrm(ks[9], (L, G, N, C), (2 * C) ** -0.5),
        'ssm_b_im': nrm(ks[10], (L, G, N, C), (2 * C) ** -0.5),
        'ssm_c_re': nrm(ks[11], (L, G, C, N), (2 * N) ** -0.5),
        'ssm_c_im': nrm(ks[12], (L, G, C, N), (2 * N) ** -0.5),
        'ssm_d': nrm(ks[13], (L, D_SSM), 0.5),
        'ssm_w_glu': nrm(ks[14], (L, D_SSM, D_SSM), D_SSM ** -0.5),
        'ssm_b_glu': nrm(ks[15], (L, D_SSM), 0.01),
        'gnorm_ssm': gain(ks[16], (L, D_SSM)),
        'gnorm_attn': gain(ks[17], (L, D_ATTN)),
        'w_out': nrm(ks[18], (L, D_MIX, D_MODEL), D_MIX ** -0.5),
        'rel_bias': nrm(ks[19], (N_BUCKETS, N_HEADS), 0.5),
        'norm_xattn': gain(ks[20], (L, D_MODEL)),
        'norm_mem': gain(ks[21], (L, D_MODEL)),
        'x_wq': nrm(ks[22], (L, D_MODEL, D_X), D_MODEL ** -0.5),
        'x_wkv': nrm(ks[23], (L, D_MODEL, 2 * D_X), D_MODEL ** -0.5),
        'x_wo': nrm(ks[24], (L, D_X, D_MODEL), D_X ** -0.5),
        'x_qn': gain(ks[25], (L, X_HEAD_DIM)),
        'x_kn': gain(ks[26], (L, X_HEAD_DIM)),
        'norm_ffn': gain(ks[27], (L, D_MODEL)),
        'ffn_w_up': nrm(ks[28], (L, D_MODEL, 2 * D_FF), D_MODEL ** -0.5),
        'ffn_conv_w': nrm(ks[29], (L, CONV_W, 1, 2 * D_FF), CONV_W ** -0.5),
        'ffn_conv_b': nrm(ks[30], (L, 2 * D_FF), 0.01),
        'ffn_w_down': nrm(ks[31], (L, D_FF, D_MODEL), D_FF ** -0.5),
    }


def reference(x, mem, norm_mix, w_in, attn_qn, attn_kn, ssm_lam_re, ssm_lam_im, ssm_log_step,
              ssm_b_re, ssm_b_im, ssm_c_re, ssm_c_im, ssm_d, ssm_w_glu, ssm_b_glu,
              gnorm_ssm, gnorm_attn, w_out, rel_bias, norm_xattn, norm_mem, x_wq, x_wkv, x_wo,
              x_qn, x_kn, norm_ffn, ffn_w_up, ffn_conv_w, ffn_conv_b, ffn_w_down):
    bsz, t = x.shape[0], x.shape[1]
    h = x
    for l in range(DEPTH):
        hn = rms_norm(h, norm_mix[l])
        proj = hn @ w_in[l]
        u_ssm = proj[..., :D_SSM]
        q = proj[..., D_SSM:D_SSM + D_ATTN].reshape(bsz, t, N_HEADS, HEAD_DIM)
        k = proj[..., D_SSM + D_ATTN:D_SSM + 2 * D_ATTN].reshape(bsz, t, N_HEADS, HEAD_DIM)
        v = proj[..., D_SSM + 2 * D_ATTN:].reshape(bsz, t, N_HEADS, HEAD_DIM)
        q = rms_norm(q, attn_qn[l])
        k = rms_norm(k, attn_kn[l])
        y_ssm = s5_mixer(u_ssm, ssm_lam_re[l], ssm_lam_im[l], ssm_log_step[l], ssm_b_re[l],
                         ssm_b_im[l], ssm_c_re[l], ssm_c_im[l], ssm_d[l], ssm_w_glu[l], ssm_b_glu[l])
        y_att = moba_attention(q, k, v, rel_bias)
        mixed = jnp.concatenate([rms_norm(y_ssm, gnorm_ssm[l]), rms_norm(y_att, gnorm_attn[l])], axis=-1)
        h = h + mixed @ w_out[l]
        h = h + memory_cross_attention(rms_norm(h, norm_xattn[l]), rms_norm(mem, norm_mem[l]),
                                       x_wq[l], x_wkv[l], x_wo[l], x_qn[l], x_kn[l])
        h = h + conv_ffn(rms_norm(h, norm_ffn[l]), ffn_w_up[l], ffn_conv_w[l], ffn_conv_b[l], ffn_w_down[l])
    return h
```

```python
import functools
import math

import numpy as np
import jax
import jax.numpy as jnp
from jax import lax
from jax.experimental import pallas as pl
from jax.experimental.pallas import tpu as pltpu

F32 = jnp.float32
BF16 = jnp.bfloat16

EPS = 1e-6
NEG = -1e30

D_SSM = 512
SSM_GROUP = 16
N_GROUPS = 32
SSM_STATE = 64
D_ATTN = 512
HEAD_DIM = 64
N_HEADS = 8
MOBA_BLOCK = 256
MOBA_TOPK = 3
N_BUCKETS = 32
MAX_DISTANCE = 128
X_HEADS = 4
X_HEAD_DIM = 128
D_X = 512
D_FF = 2816
LANES = 128
SUBLANES = 8
VMEM_LIMIT = 56 * 1024 * 1024


def _cparams(n_axes):
    return pltpu.CompilerParams(
        dimension_semantics=("arbitrary",) * n_axes, vmem_limit_bytes=VMEM_LIMIT)


def _rms_rows(x, g):
    ms = jnp.mean(x * x, axis=-1, keepdims=True)
    return x * lax.rsqrt(ms + EPS) * g


def _inproj_kernel(x_ref, g_ref, wuk_ref, wqvt_ref, qg_ref, kg_ref, pavg_ref,
                   u_ref, k_ref, kmean_ref, qt_ref, vt_ref):
    tm = x_ref.shape[1]
    hn = _rms_rows(x_ref[0], g_ref[...]).astype(BF16)
    p_uk = jnp.dot(hn, wuk_ref[...], preferred_element_type=F32)
    p_qvt = lax.dot_general(wqvt_ref[...], hn, (((1,), (1,)), ((), ())),
                            preferred_element_type=F32)
    u_ref[0] = p_uk[:, :D_SSM]
    k = p_uk[:, D_SSM:]
    k2 = k * k
    k2_hi = k2.astype(BF16)
    k2_lo = (k2 - k2_hi.astype(F32)).astype(BF16)
    ms = (jnp.dot(k2_hi, pavg_ref[...], preferred_element_type=F32)
          + jnp.dot(k2_lo, pavg_ref[...], preferred_element_type=F32))
    kn = k * lax.rsqrt(ms + EPS) * kg_ref[...]
    k_ref[0] = kn.astype(BF16)
    for r in range(tm // MOBA_BLOCK):
        kmean_ref[0, r] = jnp.mean(kn[r * MOBA_BLOCK:(r + 1) * MOBA_BLOCK], axis=0, keepdims=True)
    qt = p_qvt[:D_ATTN].reshape(N_HEADS, HEAD_DIM, tm)
    qms = jnp.mean(qt * qt, axis=1, keepdims=True)
    qn = (qt * lax.rsqrt(qms + EPS)).reshape(D_ATTN, tm)
    qn = qn * jnp.tile(qg_ref[...], (1, tm // LANES))
    vt = p_qvt[D_ATTN:]
    for r in range(tm // MOBA_BLOCK):
        cs = slice(r * MOBA_BLOCK, (r + 1) * MOBA_BLOCK)
        qt_ref[0, r] = qn[:, cs].astype(BF16)
        vt_ref[0, r] = vt[:, cs].astype(BF16)


def _inproj(x, g, w_uk, w_qvt, qg, kg, pavg, tm):
    bsz, t, d = x.shape
    nb = t // MOBA_BLOCK
    rb = tm // MOBA_BLOCK
    full = lambda shp: pl.BlockSpec(shp, lambda b, i: (0,) * len(shp))
    return pl.pallas_call(
        _inproj_kernel,
        grid=(bsz, t // tm),
        in_specs=[
            pl.BlockSpec((1, tm, d), lambda b, i: (b, i, 0)),
            full((1, d)), full(w_uk.shape), full(w_qvt.shape),
            full(qg.shape), full(kg.shape), full(pavg.shape),
        ],
        out_specs=[
            pl.BlockSpec((1, tm, D_SSM), lambda b, i: (b, i, 0)),
            pl.BlockSpec((1, tm, D_ATTN), lambda b, i: (b, i, 0)),
            pl.BlockSpec((1, rb, 1, D_ATTN), lambda b, i: (b, i, 0, 0)),
            pl.BlockSpec((1, rb, D_ATTN, MOBA_BLOCK), lambda b, i: (b, i, 0, 0)),
            pl.BlockSpec((1, rb, D_ATTN, MOBA_BLOCK), lambda b, i: (b, i, 0, 0)),
        ],
        out_shape=[
            jax.ShapeDtypeStruct((bsz, t, D_SSM), F32),
            jax.ShapeDtypeStruct((bsz, t, D_ATTN), BF16),
            jax.ShapeDtypeStruct((bsz, nb, 1, D_ATTN), F32),
            jax.ShapeDtypeStruct((bsz, nb, D_ATTN, MOBA_BLOCK), BF16),
            jax.ShapeDtypeStruct((bsz, nb, D_ATTN, MOBA_BLOCK), BF16),
        ],
        compiler_params=_cparams(2),
        name="inproj",
    )(x, g, w_uk, w_qvt, qg, kg, pavg)


def _s5prep_kernel(lre_ref, lim_ref, ls_ref, bre_ref, bim_ref, are_ref, aim_ref, bbre_ref, bbim_ref):
    a_re = jnp.minimum(lre_ref[...], -1e-4)
    a_im = lim_ref[...]
    step = jnp.exp(ls_ref[...])
    mag = jnp.exp(step * a_re)
    ab_re = mag * jnp.cos(step * a_im)
    ab_im = mag * jnp.sin(step * a_im)
    den = a_re * a_re + a_im * a_im
    p = ab_re - 1.0
    f_re = (p * a_re + ab_im * a_im) / den
    f_im = (ab_im * a_re - p * a_im) / den
    are_ref[...] = ab_re
    aim_ref[...] = ab_im
    bbre_ref[...] = f_re[:, None, :] * bre_ref[...] - f_im[:, None, :] * bim_ref[...]
    bbim_ref[...] = f_re[:, None, :] * bim_ref[...] + f_im[:, None, :] * bre_ref[...]


def _s5prep(lam_re, lam_im, log_step, b_re_t, b_im_t):
    g, n = lam_re.shape
    c = b_re_t.shape[1]
    return pl.pallas_call(
        _s5prep_kernel,
        out_shape=[jax.ShapeDtypeStruct((g, n), F32), jax.ShapeDtypeStruct((g, n), F32),
                   jax.ShapeDtypeStruct((g, c, n), F32), jax.ShapeDtypeStruct((g, c, n), F32)],
        name="s5prep",
    )(lam_re, lam_im, log_step, b_re_t, b_im_t)


S5_COLS = N_GROUPS * SSM_STATE
S5_NT = 256
S5_SCAN_COLS = 512


def _s5_kernel(u_ref, bwre_ref, bwim_ref, are_ref, aim_ref, cwre_ref, cwim_ref, d_ref, wglu_ref,
               bglu_ref, gn_ref, o_ref, sre_ref, sim_ref, stre_ref, stim_ref):
    rows = u_ref.shape[0]
    tt = rows // SUBLANES

    @pl.when(pl.program_id(0) == 0)
    def _():
        stre_ref[...] = jnp.zeros_like(stre_ref)
        stim_ref[...] = jnp.zeros_like(stim_ref)

    u = u_ref[...]
    ub = u.astype(BF16)
    for j in range(S5_COLS // S5_NT):
        slab = ub[:, (j // 2) * LANES:(j // 2 + 1) * LANES]
        cs = slice(j * S5_NT, (j + 1) * S5_NT)
        sre_ref[:, cs] = jnp.dot(slab, bwre_ref[j], preferred_element_type=F32)
        sim_ref[:, cs] = jnp.dot(slab, bwim_ref[j], preferred_element_type=F32)

    for cc in range(S5_COLS // S5_SCAN_COLS):
        cs = slice(cc * S5_SCAN_COLS, (cc + 1) * S5_SCAN_COLS)
        a_re = are_ref[:, cs]
        a_im = aim_ref[:, cs]

        def body(t, carry, cs=cs, a_re=a_re, a_im=a_im):
            s_re, s_im = carry
            r0 = pl.multiple_of(t * SUBLANES, SUBLANES)
            n_re = a_re * s_re - a_im * s_im + sre_ref[pl.ds(r0, SUBLANES), cs]
            n_im = a_re * s_im + a_im * s_re + sim_ref[pl.ds(r0, SUBLANES), cs]
            sre_ref[pl.ds(r0, SUBLANES), cs] = n_re
            sim_ref[pl.ds(r0, SUBLANES), cs] = n_im
            return n_re, n_im

        s_re, s_im = lax.fori_loop(0, tt, body, (stre_ref[:, cs], stim_ref[:, cs]), unroll=4)
        stre_ref[:, cs] = s_re
        stim_ref[:, cs] = s_im

    half = S5_COLS // 2
    ys = []
    for n in range(2):
        ks = slice(n * half, (n + 1) * half)
        y_n = (jnp.dot(sre_ref[:, ks].astype(BF16), cwre_ref[n], preferred_element_type=F32)
               - jnp.dot(sim_ref[:, ks].astype(BF16), cwim_ref[n], preferred_element_type=F32))
        ys.append(y_n)
    y = jnp.concatenate(ys, axis=1) + d_ref[...] * u
    y = jax.nn.gelu(y)
    z = jnp.dot(y.astype(BF16), wglu_ref[...], preferred_element_type=F32) + bglu_ref[...]
    y = y * jax.nn.sigmoid(z)
    o_ref[...] = _rms_rows(y, gn_ref[...]).astype(o_ref.dtype)


def _s5(u_tb, bw_re, bw_im, a_re8, a_im8, cw_re, cw_im, d, w_glu, b_glu, gn, tt):
    rows_total = u_tb.shape[0]
    rows = tt * SUBLANES
    full = lambda a: pl.BlockSpec(a.shape, lambda i: (0,) * a.ndim)
    consts = (bw_re, bw_im, a_re8, a_im8, cw_re, cw_im, d, w_glu, b_glu, gn)
    return pl.pallas_call(
        _s5_kernel,
        grid=(rows_total // rows,),
        in_specs=[pl.BlockSpec((rows, D_SSM), lambda i: (i, 0))] + [full(a) for a in consts],
        out_specs=pl.BlockSpec((rows, D_SSM), lambda i: (i, 0)),
        out_shape=jax.ShapeDtypeStruct((rows_total, D_SSM), BF16),
        scratch_shapes=[pltpu.VMEM((rows, S5_COLS), F32), pltpu.VMEM((rows, S5_COLS), F32),
                        pltpu.VMEM((SUBLANES, S5_COLS), F32), pltpu.VMEM((SUBLANES, S5_COLS), F32)],
        compiler_params=_cparams(1),
        name="s5",
    )(u_tb, *consts)


def _t5_bucket_table(n):
    d = np.arange(n)
    max_exact = N_BUCKETS // 2
    nf = np.maximum(d, max_exact).astype(np.float32)
    large = max_exact + (np.log(nf / max_exact) / math.log(MAX_DISTANCE / max_exact)
                         * (N_BUCKETS - max_exact)).astype(np.int32)
    large = np.minimum(large, N_BUCKETS - 1)
    return np.where(d < max_exact, d, large).astype(np.int32)


def _bucket_tiles():
    tbl = _t5_bucket_table(2 * MOBA_BLOCK)
    r = np.arange(MOBA_BLOCK)[:, None]
    c = np.arange(MOBA_BLOCK)[None, :]
    d_own = c - r
    own = np.where(d_own >= 0, tbl[np.maximum(d_own, 0)], -1).astype(np.int32)
    prev = tbl[d_own + MOBA_BLOCK].astype(np.int32)
    return own, prev


def _biastile_kernel(rb_ref, own_ref, prev_ref, o_ref):
    h = pl.program_id(0)
    own = own_ref[...]
    prev = prev_ref[...]
    t_own = jnp.full(own.shape, NEG, F32)
    t_prev = jnp.zeros(prev.shape, F32)
    for b in range(N_BUCKETS):
        v = rb_ref[b, h]
        t_own = jnp.where(own == b, v, t_own)
        t_prev = jnp.where(prev == b, v, t_prev)
    o_ref[0, 0] = t_own
    o_ref[0, 1] = t_prev


def _biastile(rel_bias, own, prev):
    blk = own.shape
    return pl.pallas_call(
        _biastile_kernel,
        grid=(N_HEADS,),
        in_specs=[pl.BlockSpec(memory_space=pltpu.SMEM),
                  pl.BlockSpec(blk, lambda h: (0, 0)), pl.BlockSpec(blk, lambda h: (0, 0))],
        out_specs=pl.BlockSpec((1, 2) + blk, lambda h: (h, 0, 0, 0)),
        out_shape=jax.ShapeDtypeStruct((N_HEADS, 2) + blk, F32),
        compiler_params=_cparams(1),
        name="biastile",
    )(rel_bias, own, prev)


def _moba_kernel(rb_ref, qt_ref, k_ref, vt_ref, kmean_ref, bias_ref, o_ref,
                 qpad_ref, sel_ref, m_ref, l_ref, acc_ref):
    h = pl.program_id(1)
    i = pl.program_id(2)
    nb = kmean_ref.shape[1]
    tq = qt_ref.shape[3]
    par = h % 2

    qt = qt_ref[0, 0]
    zero = jnp.zeros_like(qt)
    qpad_ref[0:HEAD_DIM] = jnp.where(par == 0, qt, zero)
    qpad_ref[HEAD_DIM:2 * HEAD_DIM] = jnp.where(par == 1, qt, zero)
    qpad = qpad_ref[...]

    gate = jnp.dot(kmean_ref[0], qpad, preferred_element_type=F32)
    blk = lax.broadcasted_iota(jnp.int32, (nb, tq), 0)
    g = jnp.where(blk < i, gate, -jnp.inf)
    sel = jnp.zeros((nb, tq), F32)
    for _ in range(MOBA_TOPK):
        mx = jnp.max(g, axis=0, keepdims=True)
        cand = jnp.where((g == mx) & (mx > -jnp.inf), blk, nb)
        idx = jnp.min(cand, axis=0, keepdims=True)
        pick = blk == idx
        sel = jnp.where(pick, 1.0, sel)
        g = jnp.where(pick, -jnp.inf, g)
    sel_ref[...] = jnp.where(sel > 0.0, 0.0, NEG)

    def scores(j):
        r0 = pl.multiple_of(j * MOBA_BLOCK, MOBA_BLOCK)
        return jnp.dot(k_ref[0, pl.ds(r0, MOBA_BLOCK), :], qpad, preferred_element_type=F32)

    s = scores(i) + bias_ref[0, 0]
    m0 = jnp.max(s, axis=0, keepdims=True)
    p = jnp.exp(s - m0)
    m_ref[...] = m0
    l_ref[...] = jnp.sum(p, axis=0, keepdims=True)
    acc_ref[...] = jnp.dot(vt_ref[0, i], p.astype(BF16), preferred_element_type=F32)

    def update(j, s):
        m_old = m_ref[...]
        m_new = jnp.maximum(m_old, jnp.max(s, axis=0, keepdims=True))
        alpha = jnp.exp(m_old - m_new)
        p = jnp.exp(s - m_new)
        m_ref[...] = m_new
        l_ref[...] = alpha * l_ref[...] + jnp.sum(p, axis=0, keepdims=True)
        acc_ref[...] = alpha * acc_ref[...] + jnp.dot(vt_ref[0, j], p.astype(BF16),
                                                      preferred_element_type=F32)

    @pl.when(i >= 1)
    def _():
        j = i - 1
        update(j, scores(j) + bias_ref[0, 1] + sel_ref[pl.ds(j, 1), :])

    far_bias = rb_ref[N_BUCKETS - 1, h]

    def far(j, carry):
        update(j, scores(j) + (sel_ref[pl.ds(j, 1), :] + far_bias))
        return carry

    lax.fori_loop(0, jnp.maximum(i - 1, 0), far, 0)

    o_ref[0, 0] = (acc_ref[...] / l_ref[...]).astype(o_ref.dtype)


def _moba(rel_bias, qt, k, vt, kmean, bias_tiles):
    bsz, nb, _, tq = qt.shape
    t = k.shape[1]
    return pl.pallas_call(
        _moba_kernel,
        grid=(bsz, N_HEADS, nb),
        in_specs=[
            pl.BlockSpec(memory_space=pltpu.SMEM),
            pl.BlockSpec((1, 1, HEAD_DIM, tq), lambda b, h, i: (b, i, h, 0)),
            pl.BlockSpec((1, t, 2 * HEAD_DIM), lambda b, h, i: (b, 0, h // 2)),
            pl.BlockSpec((1, nb, HEAD_DIM, tq), lambda b, h, i: (b, 0, h, 0)),
            pl.BlockSpec((1, nb, 2 * HEAD_DIM), lambda b, h, i: (b, 0, h // 2)),
            pl.BlockSpec((1, 2, MOBA_BLOCK, tq), lambda b, h, i: (h, 0, 0, 0)),
        ],
        out_specs=pl.BlockSpec((1, 1, HEAD_DIM, tq), lambda b, h, i: (b, i, h, 0)),
        out_shape=jax.ShapeDtypeStruct((bsz, nb, D_ATTN, tq), BF16),
        scratch_shapes=[pltpu.VMEM((2 * HEAD_DIM, tq), BF16), pltpu.VMEM((nb, tq), F32),
                        pltpu.VMEM((1, tq), F32), pltpu.VMEM((1, tq), F32),
                        pltpu.VMEM((HEAD_DIM, tq), F32)],
        compiler_params=_cparams(3),
        name="moba",
    )(rel_bias, qt, k, vt, kmean, bias_tiles)


def _memkv_kernel(mem_ref, g_ref, wkv_ref, kg_ref, k_ref, v_ref):
    mn = _rms_rows(mem_ref[0], g_ref[...]).astype(BF16)
    kv = jnp.dot(mn, wkv_ref[...], preferred_element_type=F32)
    ks = []
    for hh in range(X_HEADS):
        cs = slice(hh * X_HEAD_DIM, (hh + 1) * X_HEAD_DIM)
        ks.append(_rms_rows(kv[:, cs], kg_ref[...]))
    k_ref[0] = jnp.concatenate(ks, axis=1).astype(BF16)
    v_ref[0] = kv[:, D_X:].astype(BF16)


def _memkv(mem, g, w_kv, kg):
    bsz, m, d = mem.shape
    full = lambda a: pl.BlockSpec(a.shape, lambda b: (0,) * a.ndim)
    return pl.pallas_call(
        _memkv_kernel,
        grid=(bsz,),
        in_specs=[pl.BlockSpec((1, m, d), lambda b: (b, 0, 0)), full(g), full(w_kv), full(kg)],
        out_specs=[pl.BlockSpec((1, m, D_X), lambda b: (b, 0, 0))] * 2,
        out_shape=[jax.ShapeDtypeStruct((bsz, m, D_X), BF16)] * 2,
        compiler_params=_cparams(1),
        name="memkv",
    )(mem, g, w_kv, kg)


def _outx_kernel(x_ref, ys_ref, yat_ref, ga_ref, wout_ref, gx_ref, wq_ref, qg_ref, kx_ref, vx_ref,
                 wo_ref, o_ref):
    rb = yat_ref.shape[1]
    pieces = []
    for r in range(rb):
        ya = yat_ref[0, r].astype(F32)
        ms = jnp.mean(ya * ya, axis=0, keepdims=True)
        yan = ya * lax.rsqrt(ms + EPS) * jnp.tile(ga_ref[...], (1, ya.shape[1] // LANES))
        pieces.append(yan.T)
    ya_n = jnp.concatenate(pieces, axis=0).astype(BF16)
    h1 = (x_ref[0]
          + jnp.dot(ys_ref[0], wout_ref[:D_SSM], preferred_element_type=F32)
          + jnp.dot(ya_n, wout_ref[D_SSM:], preferred_element_type=F32))
    hn = _rms_rows(h1, gx_ref[...]).astype(BF16)
    q = jnp.dot(hn, wq_ref[...], preferred_element_type=F32)
    outs = []
    for hh in range(X_HEADS):
        cs = slice(hh * X_HEAD_DIM, (hh + 1) * X_HEAD_DIM)
        qh = _rms_rows(q[:, cs], qg_ref[...]).astype(BF16)
        s = lax.dot_general(qh, kx_ref[0, :, cs], (((1,), (1,)), ((), ())),
                            preferred_element_type=F32) * (X_HEAD_DIM ** -0.5)
        s = s - jnp.max(s, axis=-1, keepdims=True)
        e = jnp.exp(s)
        prob = e / jnp.sum(e, axis=-1, keepdims=True)
        outs.append(jnp.dot(prob.astype(BF16), vx_ref[0, :, cs], preferred_element_type=F32))
    o = jnp.concatenate(outs, axis=1).astype(BF16)
    o_ref[0] = h1 + jnp.dot(o, wo_ref[...], preferred_element_type=F32)


def _outx(x, ys, yat, ga, w_out, gx, w_q, qg, kx, vx, w_o, tm):
    bsz, t, d = x.shape
    m = kx.shape[1]
    rb = tm // MOBA_BLOCK
    full = lambda a: pl.BlockSpec(a.shape, lambda b, i: (0,) * a.ndim)
    return pl.pallas_call(
        _outx_kernel,
        grid=(bsz, t // tm),
        in_specs=[
            pl.BlockSpec((1, tm, d), lambda b, i: (b, i, 0)),
            pl.BlockSpec((1, tm, D_SSM), lambda b, i: (b, i, 0)),
            pl.BlockSpec((1, rb, D_ATTN, MOBA_BLOCK), lambda b, i: (b, i, 0, 0)),
            full(ga), full(w_out), full(gx), full(w_q), full(qg),
            pl.BlockSpec((1, m, D_X), lambda b, i: (b, 0, 0)),
            pl.BlockSpec((1, m, D_X), lambda b, i: (b, 0, 0)),
            full(w_o),
        ],
        out_specs=pl.BlockSpec((1, tm, d), lambda b, i: (b, i, 0)),
        out_shape=jax.ShapeDtypeStruct((bsz, t, d), F32),
        compiler_params=_cparams(2),
        name="outx",
    )(x, ys, yat, ga, w_out, gx, w_q, qg, kx, vx, w_o)


FFN_CHUNK = 1408


def _ffn_kernel(h_ref, g_ref, wup_ref, cw_ref, cb_ref, wdn_ref, o_ref, tail_ref):
    tm = h_ref.shape[1]

    @pl.when(pl.program_id(1) == 0)
    def _():
        tail_ref[...] = jnp.zeros_like(tail_ref)

    h = h_ref[0]
    hn = _rms_rows(h, g_ref[...]).astype(BF16)
    acc = h

    def conv(cols):
        up = jnp.dot(hn, wup_ref[:, cols], preferred_element_type=F32)
        ext = jnp.concatenate([tail_ref[:, cols], up], axis=0)
        tail_ref[:, cols] = up[tm - SUBLANES:]
        w = cw_ref[:, cols]
        return (w[0:1] * ext[SUBLANES - 2:SUBLANES - 2 + tm]
                + w[1:2] * ext[SUBLANES - 1:SUBLANES - 1 + tm]
                + w[2:3] * up + cb_ref[:, cols])

    for c in range(D_FF // FFN_CHUNK):
        gate = conv(slice(c * FFN_CHUNK, (c + 1) * FFN_CHUNK))
        val = conv(slice(D_FF + c * FFN_CHUNK, D_FF + (c + 1) * FFN_CHUNK))
        act = (jax.nn.silu(gate) * val).astype(BF16)
        acc = acc + jnp.dot(act, wdn_ref[c * FFN_CHUNK:(c + 1) * FFN_CHUNK],
                            preferred_element_type=F32)
    o_ref[0] = acc


def _ffn(h, g, w_up, conv_w, conv_b, w_down, tm):
    bsz, t, d = h.shape
    full = lambda a: pl.BlockSpec(a.shape, lambda b, i: (0,) * a.ndim, pipeline_mode=pl.Buffered(1))
    return pl.pallas_call(
        _ffn_kernel,
        grid=(bsz, t // tm),
        in_specs=[pl.BlockSpec((1, tm, d), lambda b, i: (b, i, 0)),
                  full(g), full(w_up), full(conv_w), full(conv_b), full(w_down)],
        out_specs=pl.BlockSpec((1, tm, d), lambda b, i: (b, i, 0)),
        out_shape=jax.ShapeDtypeStruct((bsz, t, d), F32),
        scratch_shapes=[pltpu.VMEM((SUBLANES, 2 * D_FF), F32)],
        compiler_params=_cparams(2),
        name="ffn",
    )(h, g, w_up, conv_w, conv_b, w_down)


def _block_diag_b(bb):
    g, c, n = bb.shape
    per_tile = S5_NT // n
    per_slab = LANES // c
    tiles = S5_COLS // S5_NT
    place = np.zeros((tiles, per_slab, per_tile), np.float32)
    for j in range(tiles):
        for q in range(per_tile):
            place[j, (j * per_tile + q) % per_slab, q] = 1.0
    out = jnp.einsum('jqcn,jgq->jgcqn', bb.reshape(tiles, per_tile, c, n), jnp.asarray(place))
    return out.reshape(tiles, LANES, S5_NT).astype(BF16)


def _block_diag_c(cm):
    g, c, n = cm.shape
    per_tile = g // 2
    out = jnp.einsum('tqcn,qp->tqnpc', cm.reshape(2, per_tile, c, n), jnp.eye(per_tile, dtype=F32))
    return out.reshape(2, per_tile * n, per_tile * c).astype(BF16)


def kernel(x, mem, norm_mix, w_in, attn_qn, attn_kn, ssm_lam_re, ssm_lam_im, ssm_log_step, ssm_b_re, ssm_b_im, ssm_c_re, ssm_c_im, ssm_d, ssm_w_glu, ssm_b_glu, gnorm_ssm, gnorm_attn, w_out, rel_bias, norm_xattn, norm_mem, x_wq, x_wkv, x_wo, x_qn, x_kn, norm_ffn, ffn_w_up, ffn_conv_w, ffn_conv_b, ffn_w_down):
    bsz, t, d = x.shape
    assert bsz == SUBLANES and t % (2 * MOBA_BLOCK) == 0
    assert norm_mix.shape[0] == 1
    l = 0
    row = lambda v: v.reshape(1, -1).astype(F32)

    w = w_in[l]
    w_uk = jnp.concatenate([w[:, :D_SSM], w[:, D_SSM + D_ATTN:D_SSM + 2 * D_ATTN]], axis=1).astype(BF16)
    w_qvt = jnp.concatenate([w[:, D_SSM:D_SSM + D_ATTN], w[:, D_SSM + 2 * D_ATTN:]], axis=1).T.astype(BF16)
    qg = jnp.broadcast_to((jnp.tile(attn_qn[l], N_HEADS) * (HEAD_DIM ** -0.5))[:, None], (D_ATTN, LANES)).astype(F32)
    kg = row(jnp.tile(attn_kn[l], N_HEADS))
    pavg = jnp.asarray(np.kron(np.eye(N_HEADS), np.full((HEAD_DIM, HEAD_DIM), 1.0 / HEAD_DIM)), BF16)
    u, k, kmean, qt, vt = _inproj(x, row(norm_mix[l]), w_uk, w_qvt, qg, kg, pavg, tm=512)

    a_re, a_im, bb_re, bb_im = _s5prep(ssm_lam_re[l], ssm_lam_im[l], ssm_log_step[l][:, None],
                                       jnp.swapaxes(ssm_b_re[l], 1, 2), jnp.swapaxes(ssm_b_im[l], 1, 2))
    a_re8 = jnp.broadcast_to(a_re.reshape(1, -1), (SUBLANES, S5_COLS))
    a_im8 = jnp.broadcast_to(a_im.reshape(1, -1), (SUBLANES, S5_COLS))
    u_tb = jnp.swapaxes(u, 0, 1).reshape(t * bsz, D_SSM)
    ys_tb = _s5(u_tb, _block_diag_b(bb_re), _block_diag_b(bb_im), a_re8, a_im8,
                _block_diag_c(ssm_c_re[l]), _block_diag_c(ssm_c_im[l]), row(ssm_d[l]),
                ssm_w_glu[l].astype(BF16), row(ssm_b_glu[l]), row(gnorm_ssm[l]), tt=128)
    ys = jnp.swapaxes(ys_tb.reshape(t, bsz, D_SSM), 0, 1)

    own, prev = _bucket_tiles()
    bias_tiles = _biastile(rel_bias, jnp.asarray(own), jnp.asarray(prev))
    nb = t // MOBA_BLOCK
    yat = _moba(rel_bias, qt, k, vt, kmean.reshape(bsz, nb, D_ATTN).astype(BF16), bias_tiles)

    kx, vx = _memkv(mem, row(norm_mem[l]), x_wkv[l].astype(BF16), row(x_kn[l]))
    ga = jnp.broadcast_to(gnorm_attn[l][:, None], (D_ATTN, LANES)).astype(F32)
    h2 = _outx(x, ys, yat, ga, w_out[l].astype(BF16), row(norm_xattn[l]), x_wq[l].astype(BF16),
               row(x_qn[l]), kx, vx, x_wo[l].astype(BF16), tm=512)

    return _ffn(h2, row(norm_ffn[l]), ffn_w_up[l].astype(BF16), ffn_conv_w[l].reshape(3, 2 * D_FF),
                row(ffn_conv_b[l]), ffn_w_down[l].astype(BF16), tm=512)
```

```python
import functools
import math

import numpy as np
import jax
import jax.numpy as jnp
from jax import lax
from jax.experimental import pallas as pl
from jax.experimental.pallas import tpu as pltpu

F32 = jnp.float32
BF16 = jnp.bfloat16

EPS = 1e-6
NEG = -1e30

D_SSM = 512
SSM_GROUP = 16
N_GROUPS = 32
SSM_STATE = 64
D_ATTN = 512
HEAD_DIM = 64
N_HEADS = 8
MOBA_BLOCK = 256
MOBA_TOPK = 3
MOBA_LOOKAHEAD = 7
MOBA_FAR_GROUP = 8
MOBA_FAR_SPAN = 1
N_BUCKETS = 32
MAX_DISTANCE = 128
X_HEADS = 4
X_HEAD_DIM = 128
D_X = 512
D_FF = 2816
LANES = 128
SUBLANES = 8
VMEM_LIMIT = 56 * 1024 * 1024


def _cparams(n_axes, flags=None):
    return pltpu.CompilerParams(
        dimension_semantics=("arbitrary",) * n_axes, vmem_limit_bytes=VMEM_LIMIT, flags=flags)


def _rms_rows(x, g):
    ms = jnp.mean(x * x, axis=-1, keepdims=True)
    return x * lax.rsqrt(ms + EPS) * g


INPROJ_PIECE = 512


def _inproj_kernel(x_ref, g_ref, wu_ref, wkqvt_ref, kg_ref, qg_ref,
                   u_ref, k_ref, kmean_ref, qt_ref, vt_ref):
    tp = INPROJ_PIECE
    pieces = range(x_ref.shape[1] // tp)
    per_piece = tp // MOBA_BLOCK
    rows = [slice(r * tp, (r + 1) * tp) for r in pieces]
    hn = [_rms_rows(x_ref[0, rows[r]], g_ref[...]).astype(BF16) for r in pieces]
    p_u, p_t = [], []
    for r in pieces:
        p_u.append(jnp.dot(hn[r], wu_ref[...], preferred_element_type=F32))
        p_t.append(lax.dot_general(wkqvt_ref[...], hn[r], (((1,), (1,)), ((), ())),
                                   preferred_element_type=F32))

    def head_norm_t(xt, gain_ref):
        x3 = xt.reshape(N_HEADS, HEAD_DIM, tp)
        ms = jnp.mean(x3 * x3, axis=1, keepdims=True)
        return (x3 * lax.rsqrt(ms + EPS)).reshape(D_ATTN, tp) * jnp.tile(gain_ref[...], (1, tp // LANES))

    for r in pieces:
        u_ref[0, rows[r]] = p_u[r]
        kn = head_norm_t(p_t[r][:D_ATTN], kg_ref).T
        k_ref[0, rows[r]] = kn.astype(BF16)
        qn = head_norm_t(p_t[r][D_ATTN:2 * D_ATTN], qg_ref)
        vt = p_t[r][2 * D_ATTN:]
        for b in range(per_piece):
            cs = slice(b * MOBA_BLOCK, (b + 1) * MOBA_BLOCK)
            qt_ref[0, r * per_piece + b] = qn[:, cs].astype(BF16)
            vt_ref[0, r * per_piece + b] = vt[:, cs].astype(BF16)
            kmean_ref[0, r * per_piece + b] = jnp.mean(kn[cs], axis=0, keepdims=True)


def _inproj(x, g, w_u, w_kqvt, kg, qg, tm):
    bsz, t, d = x.shape
    nb = t // MOBA_BLOCK
    rb = tm // MOBA_BLOCK
    full = lambda shp: pl.BlockSpec(shp, lambda b, i: (0,) * len(shp))
    return pl.pallas_call(
        _inproj_kernel,
        grid=(bsz, t // tm),
        in_specs=[
            pl.BlockSpec((1, tm, d), lambda b, i: (b, i, 0)),
            full((1, d)), full(w_u.shape), full(w_kqvt.shape), full(kg.shape), full(qg.shape),
        ],
        out_specs=[
            pl.BlockSpec((1, tm, D_SSM), lambda b, i: (b, i, 0)),
            pl.BlockSpec((1, tm, D_ATTN), lambda b, i: (b, i, 0)),
            pl.BlockSpec((1, rb, 1, D_ATTN), lambda b, i: (b, i, 0, 0)),
            pl.BlockSpec((1, rb, D_ATTN, MOBA_BLOCK), lambda b, i: (b, i, 0, 0)),
            pl.BlockSpec((1, rb, D_ATTN, MOBA_BLOCK), lambda b, i: (b, i, 0, 0)),
        ],
        out_shape=[
            jax.ShapeDtypeStruct((bsz, t, D_SSM), F32),
            jax.ShapeDtypeStruct((bsz, t, D_ATTN), BF16),
            jax.ShapeDtypeStruct((bsz, nb, 1, D_ATTN), F32),
            jax.ShapeDtypeStruct((bsz, nb, D_ATTN, MOBA_BLOCK), BF16),
            jax.ShapeDtypeStruct((bsz, nb, D_ATTN, MOBA_BLOCK), BF16),
        ],
        compiler_params=_cparams(2),
        name="inproj",
    )(x, g, w_u, w_kqvt, kg, qg)


def _s5prep_kernel(lre_ref, lim_ref, ls_ref, bre_ref, bim_ref, are_ref, aim_ref, bbre_ref, bbim_ref):
    a_re = jnp.minimum(lre_ref[...], -1e-4)
    a_im = lim_ref[...]
    step = jnp.exp(ls_ref[...])
    mag = jnp.exp(step * a_re)
    ab_re = mag * jnp.cos(step * a_im)
    ab_im = mag * jnp.sin(step * a_im)
    den = a_re * a_re + a_im * a_im
    p = ab_re - 1.0
    f_re = (p * a_re + ab_im * a_im) / den
    f_im = (ab_im * a_re - p * a_im) / den
    are_ref[...] = ab_re
    aim_ref[...] = ab_im
    bbre_ref[...] = f_re[:, None, :] * bre_ref[...] - f_im[:, None, :] * bim_ref[...]
    bbim_ref[...] = f_re[:, None, :] * bim_ref[...] + f_im[:, None, :] * bre_ref[...]


def _s5prep(lam_re, lam_im, log_step, b_re_t, b_im_t):
    g, n = lam_re.shape
    c = b_re_t.shape[1]
    return pl.pallas_call(
        _s5prep_kernel,
        out_shape=[jax.ShapeDtypeStruct((g, n), F32), jax.ShapeDtypeStruct((g, n), F32),
                   jax.ShapeDtypeStruct((g, c, n), F32), jax.ShapeDtypeStruct((g, c, n), F32)],
        name="s5prep",
    )(lam_re, lam_im, log_step, b_re_t, b_im_t)


S5_COLS = N_GROUPS * SSM_STATE
S5_NT = 256
S5_SCAN_COLS = 512
S5_PARTS = 4


def _s5_kernel(u_ref, bwre_ref, bwim_ref, are_ref, aim_ref, cwre_ref, cwim_ref, d_ref, wglu_ref,
               bglu_ref, gn_ref, o_ref, sre_ref, sim_ref, stre_ref, stim_ref, ut_ref, yt_ref):
    bsz, tt, _ = u_ref.shape
    n_slab = D_SSM // LANES

    @pl.when(pl.program_id(0) == 0)
    def _():
        stre_ref[...] = jnp.zeros_like(stre_ref)
        stim_ref[...] = jnp.zeros_like(stim_ref)

    for b in range(bsz):
        for s in range(n_slab):
            ut_ref[s, pl.ds(b, tt, stride=bsz), :] = u_ref[b, :, s * LANES:(s + 1) * LANES]
    n_parts = S5_PARTS
    tq = tt // n_parts
    n_chunks = S5_COLS // S5_SCAN_COLS
    chunks = [slice(cc * S5_SCAN_COLS, (cc + 1) * S5_SCAN_COLS) for cc in range(n_chunks)]
    a_re = [are_ref[:, cs] for cs in chunks]
    a_im = [aim_ref[:, cs] for cs in chunks]
    state = [(stre_ref[:, cs], stim_ref[:, cs]) for cs in chunks]
    half = S5_COLS // 2

    def part_rows(q):
        return slice(q * tq * bsz, (q + 1) * tq * bsz)

    def bu(q):
        rs = part_rows(q)
        slabs = [ut_ref[s, rs, :].astype(BF16) for s in range(n_slab)]
        for j in range(S5_COLS // S5_NT):
            cs = slice(j * S5_NT, (j + 1) * S5_NT)
            sre_ref[rs, cs] = jnp.dot(slabs[j // 2], bwre_ref[j], preferred_element_type=F32)
            sim_ref[rs, cs] = jnp.dot(slabs[j // 2], bwim_ref[j], preferred_element_type=F32)

    def scan(q):
        for cc, cs in enumerate(chunks):
            s_re, s_im = state[cc]
            for t in range(q * tq, (q + 1) * tq):
                rs = slice(t * bsz, (t + 1) * bsz)
                n_re = a_re[cc] * s_re - a_im[cc] * s_im + sre_ref[rs, cs]
                n_im = a_re[cc] * s_im + a_im[cc] * s_re + sim_ref[rs, cs]
                sre_ref[rs, cs] = n_re
                sim_ref[rs, cs] = n_im
                s_re, s_im = n_re, n_im
            state[cc] = (s_re, s_im)

    y_parts = {}

    def cs_out(q):
        rs = part_rows(q)
        ys = []
        for n in range(2):
            ks = slice(n * half, (n + 1) * half)
            ys.append(jnp.dot(sre_ref[rs, ks].astype(BF16), cwre_ref[n], preferred_element_type=F32)
                      - jnp.dot(sim_ref[rs, ks].astype(BF16), cwim_ref[n], preferred_element_type=F32))
        u = jnp.concatenate([ut_ref[s, rs, :] for s in range(n_slab)], axis=1)
        y_parts[q] = jax.nn.gelu(jnp.concatenate(ys, axis=1) + d_ref[...] * u)

    def glu(q):
        rs = part_rows(q)
        y = y_parts.pop(q)
        z = jnp.dot(y.astype(BF16), wglu_ref[...], preferred_element_type=F32) + bglu_ref[...]
        yn = _rms_rows(y * jax.nn.sigmoid(z), gn_ref[...])
        for s in range(n_slab):
            yt_ref[s, rs, :] = yn[:, s * LANES:(s + 1) * LANES]

    bu(0)
    for q in range(n_parts + 3):
        if q + 1 < n_parts:
            bu(q + 1)
        if q < n_parts:
            scan(q)
        if 0 <= q - 1 < n_parts:
            cs_out(q - 1)
        if 0 <= q - 2 < n_parts:
            glu(q - 2)
    for cc, cs in enumerate(chunks):
        stre_ref[:, cs], stim_ref[:, cs] = state[cc]

    for b in range(bsz):
        for s in range(n_slab):
            o_ref[b, :, s * LANES:(s + 1) * LANES] = yt_ref[s, pl.ds(b, tt, stride=bsz), :].astype(o_ref.dtype)


def _s5(u, bw_re, bw_im, a_re8, a_im8, cw_re, cw_im, d, w_glu, b_glu, gn, tt):
    bsz, t, _ = u.shape
    assert bsz == SUBLANES
    rows = tt * bsz
    full = lambda a: pl.BlockSpec(a.shape, lambda i: (0,) * a.ndim)
    consts = (bw_re, bw_im, a_re8, a_im8, cw_re, cw_im, d, w_glu, b_glu, gn)
    slab_rows = pltpu.VMEM((D_SSM // LANES, rows, LANES), F32)
    return pl.pallas_call(
        _s5_kernel,
        grid=(t // tt,),
        in_specs=[pl.BlockSpec((bsz, tt, D_SSM), lambda i: (0, i, 0))] + [full(a) for a in consts],
        out_specs=pl.BlockSpec((bsz, tt, D_SSM), lambda i: (0, i, 0)),
        out_shape=jax.ShapeDtypeStruct((bsz, t, D_SSM), BF16),
        scratch_shapes=[pltpu.VMEM((rows, S5_COLS), F32), pltpu.VMEM((rows, S5_COLS), F32),
                        pltpu.VMEM((bsz, S5_COLS), F32), pltpu.VMEM((bsz, S5_COLS), F32),
                        slab_rows, slab_rows],
        compiler_params=_cparams(1),
        name="s5",
    )(u, *consts)


def _t5_bucket_table(n):
    d = np.arange(n)
    max_exact = N_BUCKETS // 2
    nf = np.maximum(d, max_exact).astype(np.float32)
    large = max_exact + (np.log(nf / max_exact) / math.log(MAX_DISTANCE / max_exact)
                         * (N_BUCKETS - max_exact)).astype(np.int32)
    large = np.minimum(large, N_BUCKETS - 1)
    return np.where(d < max_exact, d, large).astype(np.int32)


def _bucket_tiles():
    tbl = _t5_bucket_table(2 * MOBA_BLOCK)
    r = np.arange(MOBA_BLOCK)[:, None]
    c = np.arange(MOBA_BLOCK)[None, :]
    d_own = c - r
    own = np.where(d_own >= 0, tbl[np.maximum(d_own, 0)], -1).astype(np.int32)
    prev = tbl[d_own + MOBA_BLOCK].astype(np.int32)
    return own, prev


def _biastile_kernel(rb_ref, own_ref, prev_ref, o_ref):
    h = pl.program_id(0)
    own = own_ref[...]
    prev = prev_ref[...]
    t_own = jnp.full(own.shape, NEG, F32)
    t_prev = jnp.zeros(prev.shape, F32)
    for b in range(N_BUCKETS):
        v = rb_ref[b, h]
        t_own = jnp.where(own == b, v, t_own)
        t_prev = jnp.where(prev == b, v, t_prev)
    o_ref[0, 0] = t_own
    o_ref[0, 1] = t_prev


def _biastile(rel_bias, own, prev):
    blk = own.shape
    return pl.pallas_call(
        _biastile_kernel,
        grid=(N_HEADS,),
        in_specs=[pl.BlockSpec(memory_space=pltpu.SMEM),
                  pl.BlockSpec(blk, lambda h: (0, 0)), pl.BlockSpec(blk, lambda h: (0, 0))],
        out_specs=pl.BlockSpec((1, 2) + blk, lambda h: (h, 0, 0, 0)),
        out_shape=jax.ShapeDtypeStruct((N_HEADS, 2) + blk, F32),
        compiler_params=_cparams(1),
        name="biastile",
    )(rel_bias, own, prev)


MOBA_BIAS_PARTS = 3
MOBA_ONES_ROWS = 16


def _moba_selector(nb):
    e = np.zeros((nb, MOBA_BLOCK, LANES), np.float32)
    for j in range(nb):
        e[j, :, j] = 1.0
        e[j, :, nb:nb + MOBA_BIAS_PARTS] = 1.0
    return e


def _moba_kernel(rb_ref, qt_ref, k_ref, vt_ref, kmean_ref, bias_ref, esel_ref, o_ref,
                 qpad_ref, sel_ref, m_ref, acc_ref):
    i = pl.program_id(1)
    nb = kmean_ref.shape[1]
    tq = qt_ref.shape[3]
    pair_w = 2 * HEAD_DIM
    heads = range(N_HEADS)

    def pair_cols(h):
        return slice((h // 2) * pair_w, (h // 2 + 1) * pair_w)

    blk = lax.broadcasted_iota(jnp.int32, (nb, tq), 0)
    for h in heads:
        qt = qt_ref[0, 0, h * HEAD_DIM:(h + 1) * HEAD_DIM, :]
        zero = jnp.zeros_like(qt)
        qpad = jnp.concatenate([qt, zero] if h % 2 == 0 else [zero, qt], axis=0)
        gate = jnp.dot(kmean_ref[0, :, pair_cols(h)], qpad, preferred_element_type=F32)
        g = jnp.where(blk < i, gate, -jnp.inf)
        sel = jnp.zeros((nb, tq), F32)
        for _ in range(MOBA_TOPK):
            mx = jnp.max(g, axis=0, keepdims=True)
            cand = jnp.where((g == mx) & (mx > -jnp.inf), blk, nb)
            idx = jnp.min(cand, axis=0, keepdims=True)
            pick = blk == idx
            sel = jnp.where(pick, 1.0, sel)
            g = jnp.where(pick, -jnp.inf, g)
        mask = jnp.where(sel > 0.0, 0.0, NEG)
        sel_ref[h] = mask
        c = jnp.full((nb, tq), rb_ref[N_BUCKETS - 1, h], F32)
        cb = jnp.zeros((nb, tq), F32)
        for r in range(MOBA_BIAS_PARTS):
            pc = c.astype(BF16).astype(F32)
            cb = jnp.where(blk == r, pc, cb)
            c = c - pc
        pad = jnp.zeros((pair_w - 2 * nb, tq), BF16)
        qpad_ref[h] = jnp.concatenate([qpad, mask.astype(BF16), cb.astype(BF16), pad], axis=0)

    def key_block(h, j):
        r0 = pl.multiple_of(j * MOBA_BLOCK, MOBA_BLOCK)
        return k_ref[0, pl.ds(r0, MOBA_BLOCK), pair_cols(h)]

    def near_scores(h, j, bias):
        return jnp.dot(key_block(h, j), qpad_ref[h, 0:pair_w], preferred_element_type=F32) + bias

    def far_scores(h, j, n_blk):
        r0 = pl.multiple_of(j * MOBA_BLOCK, MOBA_BLOCK)
        keys = k_ref[0, pl.ds(r0, n_blk * MOBA_BLOCK), pair_cols(h)]
        sel = esel_ref[pl.ds(j, n_blk)].reshape(n_blk * MOBA_BLOCK, LANES)
        lhs = jnp.concatenate([keys, sel], axis=1)
        return jnp.dot(lhs, qpad_ref[h], preferred_element_type=F32)

    def tile_pass(tiles, state):
        queue = {}
        issued = 0
        ahead = 0
        for t in range(len(tiles)):
            while issued < len(tiles) and (issued == t or ahead + tiles[issued][2] <= MOBA_LOOKAHEAD):
                queue[issued] = tiles[issued][3]()
                ahead += tiles[issued][2]
                issued += 1
            h, j, n_blk, _ = tiles[t]
            s = queue.pop(t)
            ahead -= n_blk
            m_new = jnp.max(s, axis=0, keepdims=True)
            if h in state:
                m_new = jnp.maximum(state[h][0], m_new)
            p = jnp.exp(s - m_new).astype(BF16)
            vt = [vt_ref[0, j + jj, h * HEAD_DIM:(h + 1) * HEAD_DIM, :] for jj in range(n_blk)]
            v1 = jnp.concatenate([jnp.concatenate(vt, axis=1),
                                  jnp.ones((MOBA_ONES_ROWS, n_blk * MOBA_BLOCK), BF16)], axis=0)
            acc_new = jnp.dot(v1, p, preferred_element_type=F32)
            if h in state:
                acc_new = jnp.exp(state[h][0] - m_new) * state[h][1] + acc_new
            state[h] = (m_new, acc_new)
        return state

    def load_state():
        return {h: (m_ref[h], acc_ref[h]) for h in heads}

    def store_state(state):
        for h in heads:
            m_ref[h], acc_ref[h] = state[h]

    def own_tiles():
        return [(h, i, 1, functools.partial(near_scores, h, i, bias_ref[h, 0])) for h in heads]

    @pl.when(i == 0)
    def _():
        store_state(tile_pass(own_tiles(), {}))

    def far_tiles(j0, count):
        spans = [(jj, MOBA_FAR_SPAN) for jj in range(0, count - count % MOBA_FAR_SPAN, MOBA_FAR_SPAN)]
        spans += [(jj, 1) for jj in range(count - count % MOBA_FAR_SPAN, count)]
        return [(h, j0 + jj, n, functools.partial(far_scores, h, j0 + jj, n)) for jj, n in spans for h in heads]

    n_far = jnp.maximum(i - 1, 0)
    rem = n_far % MOBA_FAR_GROUP
    for r in range(MOBA_FAR_GROUP):
        @pl.when((i >= 1) & (rem == r))
        def _(r=r):
            j = i - 1
            prev = [(h, j, 1, functools.partial(near_scores, h, j, bias_ref[h, 1] + sel_ref[h, pl.ds(j, 1), :]))
                    for h in heads]
            store_state(tile_pass(own_tiles() + prev + far_tiles(0, r), {}))

    def far_group(it, flat):
        state = tile_pass(far_tiles(rem + MOBA_FAR_GROUP * it, MOBA_FAR_GROUP), {h: flat[h] for h in heads})
        return tuple(state[h] for h in heads)

    state0 = load_state()
    final = lax.fori_loop(0, n_far // MOBA_FAR_GROUP, far_group, tuple(state0[h] for h in heads))

    for h in heads:
        acc = final[h][1]
        o_ref[0, 0, h * HEAD_DIM:(h + 1) * HEAD_DIM, :] = (
            acc[:HEAD_DIM] / acc[HEAD_DIM:HEAD_DIM + 1]).astype(o_ref.dtype)


def _moba(rel_bias, qt, k, vt, kmean, bias_tiles):
    bsz, nb, _, tq = qt.shape
    t = k.shape[1]
    esel = jnp.asarray(_moba_selector(nb), BF16)
    once = lambda a: pl.BlockSpec(a.shape, lambda b, i: (0,) * a.ndim, pipeline_mode=pl.Buffered(1))
    return pl.pallas_call(
        _moba_kernel,
        grid=(bsz, nb),
        in_specs=[
            pl.BlockSpec(memory_space=pltpu.SMEM),
            pl.BlockSpec((1, 1, D_ATTN, tq), lambda b, i: (b, i, 0, 0)),
            pl.BlockSpec((1, t, D_ATTN), lambda b, i: (b, 0, 0)),
            pl.BlockSpec((1, nb, D_ATTN, tq), lambda b, i: (b, 0, 0, 0)),
            pl.BlockSpec((1, nb, D_ATTN), lambda b, i: (b, 0, 0)),
            once(bias_tiles), once(esel),
        ],
        out_specs=pl.BlockSpec((1, 1, D_ATTN, tq), lambda b, i: (b, i, 0, 0)),
        out_shape=jax.ShapeDtypeStruct((bsz, nb, D_ATTN, tq), BF16),
        scratch_shapes=[pltpu.VMEM((N_HEADS, 4 * HEAD_DIM, tq), BF16), pltpu.VMEM((N_HEADS, nb, tq), F32),
                        pltpu.VMEM((N_HEADS, 1, tq), F32),
                        pltpu.VMEM((N_HEADS, HEAD_DIM + MOBA_ONES_ROWS, tq), F32)],
        compiler_params=_cparams(2),
        name="moba",
    )(rel_bias, qt, k, vt, kmean, bias_tiles, esel)


def _memkv_kernel(mem_ref, g_ref, wkv_ref, kg_ref, k_ref, v_ref):
    mn = _rms_rows(mem_ref[0], g_ref[...]).astype(BF16)
    kv = jnp.dot(mn, wkv_ref[...], preferred_element_type=F32)
    ks = []
    for hh in range(X_HEADS):
        cs = slice(hh * X_HEAD_DIM, (hh + 1) * X_HEAD_DIM)
        ks.append(_rms_rows(kv[:, cs], kg_ref[...]))
    k_ref[0] = jnp.concatenate(ks, axis=1).astype(BF16)
    v_ref[0] = kv[:, D_X:].astype(BF16)


def _memkv(mem, g, w_kv, kg):
    bsz, m, d = mem.shape
    full = lambda a: pl.BlockSpec(a.shape, lambda b: (0,) * a.ndim)
    return pl.pallas_call(
        _memkv_kernel,
        grid=(bsz,),
        in_specs=[pl.BlockSpec((1, m, d), lambda b: (b, 0, 0)), full(g), full(w_kv), full(kg)],
        out_specs=[pl.BlockSpec((1, m, D_X), lambda b: (b, 0, 0))] * 2,
        out_shape=[jax.ShapeDtypeStruct((bsz, m, D_X), BF16)] * 2,
        compiler_params=_cparams(1),
        name="memkv",
    )(mem, g, w_kv, kg)


OUTX_PIECE = 512


def _outx_kernel(x_ref, ys_ref, yat_ref, ga_ref, wout_ref, gx_ref, wq_ref, qg_ref, kx_ref, vx_ref,
                 wo_ref, o_ref):
    per_piece = OUTX_PIECE // MOBA_BLOCK
    pieces = range(yat_ref.shape[1] // per_piece)
    rows = [slice(r * OUTX_PIECE, (r + 1) * OUTX_PIECE) for r in pieces]
    ya_n = []
    for r in pieces:
        parts = []
        for blk in range(r * per_piece, (r + 1) * per_piece):
            ya = yat_ref[0, blk].astype(F32)
            ms = jnp.mean(ya * ya, axis=0, keepdims=True)
            yan = ya * lax.rsqrt(ms + EPS) * jnp.tile(ga_ref[...], (1, ya.shape[1] // LANES))
            parts.append(yan.T)
        ya_n.append(jnp.concatenate(parts, axis=0).astype(BF16))
    h1 = [x_ref[0, rows[r]]
          + jnp.dot(ys_ref[0, rows[r]], wout_ref[:D_SSM], preferred_element_type=F32)
          + jnp.dot(ya_n[r], wout_ref[D_SSM:], preferred_element_type=F32) for r in pieces]
    hn = [_rms_rows(h1[r], gx_ref[...]).astype(BF16) for r in pieces]
    q = [jnp.dot(hn[r], wq_ref[...], preferred_element_type=F32) for r in pieces]
    outs = [[] for _ in pieces]
    head_cols = [slice(hh * X_HEAD_DIM, (hh + 1) * X_HEAD_DIM) for hh in range(X_HEADS)]

    def scores(hh):
        cs = head_cols[hh]
        qh = [_rms_rows(q[r][:, cs], qg_ref[...]).astype(BF16) for r in pieces]
        return [lax.dot_general(qh[r], kx_ref[0, :, cs], (((1,), (1,)), ((), ())),
                                preferred_element_type=F32) * (X_HEAD_DIM ** -0.5) for r in pieces]

    s_next = scores(0)
    for hh in range(X_HEADS):
        s = s_next
        if hh + 1 < X_HEADS:
            s_next = scores(hh + 1)
        for r in pieces:
            e = jnp.exp(s[r] - jnp.max(s[r], axis=-1, keepdims=True))
            prob = e / jnp.sum(e, axis=-1, keepdims=True)
            outs[r].append(jnp.dot(prob.astype(BF16), vx_ref[0, :, head_cols[hh]],
                                   preferred_element_type=F32))
    for r in pieces:
        o = jnp.concatenate(outs[r], axis=1).astype(BF16)
        o_ref[0, rows[r]] = h1[r] + jnp.dot(o, wo_ref[...], preferred_element_type=F32)


def _outx(x, ys, yat, ga, w_out, gx, w_q, qg, kx, vx, w_o, tm):
    bsz, t, d = x.shape
    m = kx.shape[1]
    rb = tm // MOBA_BLOCK
    full = lambda a: pl.BlockSpec(a.shape, lambda b, i: (0,) * a.ndim)
    return pl.pallas_call(
        _outx_kernel,
        grid=(bsz, t // tm),
        in_specs=[
            pl.BlockSpec((1, tm, d), lambda b, i: (b, i, 0)),
            pl.BlockSpec((1, tm, D_SSM), lambda b, i: (b, i, 0)),
            pl.BlockSpec((1, rb, D_ATTN, MOBA_BLOCK), lambda b, i: (b, i, 0, 0)),
            full(ga), full(w_out), full(gx), full(w_q), full(qg),
            pl.BlockSpec((1, m, D_X), lambda b, i: (b, 0, 0)),
            pl.BlockSpec((1, m, D_X), lambda b, i: (b, 0, 0)),
            full(w_o),
        ],
        out_specs=pl.BlockSpec((1, tm, d), lambda b, i: (b, i, 0)),
        out_shape=jax.ShapeDtypeStruct((bsz, t, d), F32),
        compiler_params=_cparams(2),
        name="outx",
    )(x, ys, yat, ga, w_out, gx, w_q, qg, kx, vx, w_o)


FFN_CHUNK = 1408


def _ffn_kernel(h_ref, g_ref, wup_ref, cw_ref, cb_ref, wdn_ref, o_ref, tail_ref):
    tm = h_ref.shape[1]

    @pl.when(pl.program_id(1) == 0)
    def _():
        tail_ref[...] = jnp.zeros_like(tail_ref)

    h = h_ref[0]
    hn = _rms_rows(h, g_ref[...]).astype(BF16)
    acc = h

    def conv(up, cols):
        ext = jnp.concatenate([tail_ref[:, cols], up], axis=0)
        tail_ref[:, cols] = up[tm - SUBLANES:]
        w = cw_ref[:, cols]
        return (w[0:1] * ext[SUBLANES - 2:SUBLANES - 2 + tm]
                + w[1:2] * ext[SUBLANES - 1:SUBLANES - 1 + tm]
                + w[2:3] * up + cb_ref[:, cols])

    n_chunks = D_FF // FFN_CHUNK
    cols = [(slice(c * FFN_CHUNK, (c + 1) * FFN_CHUNK),
             slice(D_FF + c * FFN_CHUNK, D_FF + (c + 1) * FFN_CHUNK)) for c in range(n_chunks)]
    ups = [tuple(jnp.dot(hn, wup_ref[:, cs], preferred_element_type=F32) for cs in cols[c])
           for c in range(n_chunks)]
    for c in range(n_chunks):
        act = (jax.nn.silu(conv(ups[c][0], cols[c][0])) * conv(ups[c][1], cols[c][1])).astype(BF16)
        acc = acc + jnp.dot(act, wdn_ref[cols[c][0]], preferred_element_type=F32)
    o_ref[0] = acc


def _ffn(h, g, w_up, conv_w, conv_b, w_down, tm):
    bsz, t, d = h.shape
    full = lambda a: pl.BlockSpec(a.shape, lambda b, i: (0,) * a.ndim, pipeline_mode=pl.Buffered(1))
    return pl.pallas_call(
        _ffn_kernel,
        grid=(bsz, t // tm),
        in_specs=[pl.BlockSpec((1, tm, d), lambda b, i: (b, i, 0)),
                  full(g), full(w_up), full(conv_w), full(conv_b), full(w_down)],
        out_specs=pl.BlockSpec((1, tm, d), lambda b, i: (b, i, 0)),
        out_shape=jax.ShapeDtypeStruct((bsz, t, d), F32),
        scratch_shapes=[pltpu.VMEM((SUBLANES, 2 * D_FF), F32)],
        compiler_params=_cparams(2),
        name="ffn",
    )(h, g, w_up, conv_w, conv_b, w_down)


def _block_diag_b(bb):
    g, c, n = bb.shape
    per_tile = S5_NT // n
    per_slab = LANES // c
    tiles = S5_COLS // S5_NT
    place = np.zeros((tiles, per_slab, per_tile), np.float32)
    for j in range(tiles):
        for q in range(per_tile):
            place[j, (j * per_tile + q) % per_slab, q] = 1.0
    out = jnp.einsum('jqcn,jgq->jgcqn', bb.reshape(tiles, per_tile, c, n), jnp.asarray(place))
    return out.reshape(tiles, LANES, S5_NT).astype(BF16)


def _block_diag_c(cm):
    g, c, n = cm.shape
    per_tile = g // 2
    out = jnp.einsum('tqcn,qp->tqnpc', cm.reshape(2, per_tile, c, n), jnp.eye(per_tile, dtype=F32))
    return out.reshape(2, per_tile * n, per_tile * c).astype(BF16)


def kernel(x, mem, norm_mix, w_in, attn_qn, attn_kn, ssm_lam_re, ssm_lam_im, ssm_log_step, ssm_b_re, ssm_b_im, ssm_c_re, ssm_c_im, ssm_d, ssm_w_glu, ssm_b_glu, gnorm_ssm, gnorm_attn, w_out, rel_bias, norm_xattn, norm_mem, x_wq, x_wkv, x_wo, x_qn, x_kn, norm_ffn, ffn_w_up, ffn_conv_w, ffn_conv_b, ffn_w_down):
    bsz, t, d = x.shape
    assert bsz == SUBLANES and t % (2 * max(INPROJ_PIECE, OUTX_PIECE)) == 0
    assert norm_mix.shape[0] == 1
    l = 0
    row = lambda v: v.reshape(1, -1).astype(F32)

    w = w_in[l]
    w_u = w[:, :D_SSM].astype(BF16)
    w_kqvt = jnp.concatenate([w[:, D_SSM + D_ATTN:D_SSM + 2 * D_ATTN], w[:, D_SSM:D_SSM + D_ATTN],
                              w[:, D_SSM + 2 * D_ATTN:]], axis=1).T.astype(BF16)
    lane_rep = lambda v: jnp.broadcast_to(v[:, None], (v.shape[0], LANES)).astype(F32)
    qg = lane_rep(jnp.tile(attn_qn[l], N_HEADS) * (HEAD_DIM ** -0.5))
    kg = lane_rep(jnp.tile(attn_kn[l], N_HEADS))
    u, k, kmean, qt, vt = _inproj(x, row(norm_mix[l]), w_u, w_kqvt, kg, qg, tm=2 * INPROJ_PIECE)

    a_re, a_im, bb_re, bb_im = _s5prep(ssm_lam_re[l], ssm_lam_im[l], ssm_log_step[l][:, None],
                                       jnp.swapaxes(ssm_b_re[l], 1, 2), jnp.swapaxes(ssm_b_im[l], 1, 2))
    a_re8 = jnp.broadcast_to(a_re.reshape(1, -1), (SUBLANES, S5_COLS))
    a_im8 = jnp.broadcast_to(a_im.reshape(1, -1), (SUBLANES, S5_COLS))
    ys = _s5(u, _block_diag_b(bb_re), _block_diag_b(bb_im), a_re8, a_im8,
             _block_diag_c(ssm_c_re[l]), _block_diag_c(ssm_c_im[l]), row(ssm_d[l]),
             ssm_w_glu[l].astype(BF16), row(ssm_b_glu[l]), row(gnorm_ssm[l]), tt=128)

    own, prev = _bucket_tiles()
    bias_tiles = _biastile(rel_bias, jnp.asarray(own), jnp.asarray(prev))
    nb = t // MOBA_BLOCK
    yat = _moba(rel_bias, qt, k, vt, kmean.reshape(bsz, nb, D_ATTN).astype(BF16), bias_tiles)

    kx, vx = _memkv(mem, row(norm_mem[l]), x_wkv[l].astype(BF16), row(x_kn[l]))
    ga = jnp.broadcast_to(gnorm_attn[l][:, None], (D_ATTN, LANES)).astype(F32)
    h2 = _outx(x, ys, yat, ga, w_out[l].astype(BF16), row(norm_xattn[l]), x_wq[l].astype(BF16),
               row(x_qn[l]), kx, vx, x_wo[l].astype(BF16), tm=2 * OUTX_PIECE)

    return _ffn(h2, row(norm_ffn[l]), ffn_w_up[l].astype(BF16), ffn_conv_w[l].reshape(3, 2 * D_FF),
                row(ffn_conv_b[l]), ffn_w_down[l].astype(BF16), tm=512)
```

```python
import functools
import math

import numpy as np
import jax
import jax.numpy as jnp
from jax import lax
from jax.experimental import pallas as pl
from jax.experimental.pallas import tpu as pltpu

F32 = jnp.float32
BF16 = jnp.bfloat16

EPS = 1e-6
NEG = -1e30

D_SSM = 512
N_GROUPS = 32
SSM_STATE = 64
D_ATTN = 512
HEAD_DIM = 64
N_HEADS = 8
MOBA_BLOCK = 256
MOBA_TOPK = 3
MOBA_LOOKAHEAD = 6
MOBA_FAR_GROUP = 8
N_BUCKETS = 32
MAX_DISTANCE = 128
X_HEADS = 4
X_HEAD_DIM = 128
D_X = 512
D_FF = 2816
LANES = 128
SUBLANES = 8
V7X_VMEM_BYTES = 64 * 1024 * 1024
VMEM_LIMIT = V7X_VMEM_BYTES * 7 // 8

S5_TIME_TILE = 128
FFN_ROWS = 512


def _cparams(n_axes):
    return pltpu.CompilerParams(
        dimension_semantics=("arbitrary",) * n_axes, vmem_limit_bytes=VMEM_LIMIT)


def _rms_rows(x, g):
    ms = jnp.mean(x * x, axis=-1, keepdims=True)
    return x * lax.rsqrt(ms + EPS) * g


INPROJ_PIECE = 512


def _inproj_kernel(x_ref, g_ref, wu_ref, wkqvt_ref, kg_ref, qg_ref,
                   u_ref, k_ref, kmean_ref, qt_ref, vt_ref):
    tp = INPROJ_PIECE
    pieces = range(x_ref.shape[1] // tp)
    per_piece = tp // MOBA_BLOCK
    rows = [slice(r * tp, (r + 1) * tp) for r in pieces]
    hn = [_rms_rows(x_ref[0, rows[r]], g_ref[...]).astype(BF16) for r in pieces]
    p_u, p_t = [], []
    for r in pieces:
        p_u.append(jnp.dot(hn[r], wu_ref[...], preferred_element_type=F32))
        p_t.append(lax.dot_general(wkqvt_ref[...], hn[r], (((1,), (1,)), ((), ())),
                                   preferred_element_type=F32))

    def head_norm_t(xt, gain_ref):
        x3 = xt.reshape(N_HEADS, HEAD_DIM, tp)
        ms = jnp.mean(x3 * x3, axis=1, keepdims=True)
        return (x3 * lax.rsqrt(ms + EPS)).reshape(D_ATTN, tp) * jnp.tile(gain_ref[...], (1, tp // LANES))

    for r in pieces:
        u_ref[0, rows[r]] = p_u[r]
        kn = head_norm_t(p_t[r][:D_ATTN], kg_ref).T
        k_ref[0, rows[r]] = kn.astype(BF16)
        qn = head_norm_t(p_t[r][D_ATTN:2 * D_ATTN], qg_ref)
        vt = p_t[r][2 * D_ATTN:]
        for b in range(per_piece):
            cs = slice(b * MOBA_BLOCK, (b + 1) * MOBA_BLOCK)
            qt_ref[0, r * per_piece + b] = qn[:, cs].astype(BF16)
            vt_ref[0, r * per_piece + b] = vt[:, cs].astype(BF16)
            kmean_ref[0, r * per_piece + b] = jnp.mean(kn[cs], axis=0, keepdims=True)


def _inproj(x, g, w_u, w_kqvt, kg, qg, tm):
    bsz, t, d = x.shape
    nb = t // MOBA_BLOCK
    rb = tm // MOBA_BLOCK
    full = lambda shp: pl.BlockSpec(shp, lambda b, i: (0,) * len(shp))
    return pl.pallas_call(
        _inproj_kernel,
        grid=(bsz, t // tm),
        in_specs=[
            pl.BlockSpec((1, tm, d), lambda b, i: (b, i, 0)),
            full((1, d)), full(w_u.shape), full(w_kqvt.shape), full(kg.shape), full(qg.shape),
        ],
        out_specs=[
            pl.BlockSpec((1, tm, D_SSM), lambda b, i: (b, i, 0)),
            pl.BlockSpec((1, tm, D_ATTN), lambda b, i: (b, i, 0)),
            pl.BlockSpec((1, rb, 1, D_ATTN), lambda b, i: (b, i, 0, 0)),
            pl.BlockSpec((1, rb, D_ATTN, MOBA_BLOCK), lambda b, i: (b, i, 0, 0)),
            pl.BlockSpec((1, rb, D_ATTN, MOBA_BLOCK), lambda b, i: (b, i, 0, 0)),
        ],
        out_shape=[
            jax.ShapeDtypeStruct((bsz, t, D_SSM), F32),
            jax.ShapeDtypeStruct((bsz, t, D_ATTN), BF16),
            jax.ShapeDtypeStruct((bsz, nb, 1, D_ATTN), F32),
            jax.ShapeDtypeStruct((bsz, nb, D_ATTN, MOBA_BLOCK), BF16),
            jax.ShapeDtypeStruct((bsz, nb, D_ATTN, MOBA_BLOCK), BF16),
        ],
        compiler_params=_cparams(2),
        name="inproj",
    )(x, g, w_u, w_kqvt, kg, qg)


def _s5prep_kernel(lre_ref, lim_ref, ls_ref, bre_ref, bim_ref, are_ref, aim_ref, bbre_ref, bbim_ref):
    a_re = jnp.minimum(lre_ref[...], -1e-4)
    a_im = lim_ref[...]
    step = jnp.exp(ls_ref[...])
    mag = jnp.exp(step * a_re)
    ab_re = mag * jnp.cos(step * a_im)
    ab_im = mag * jnp.sin(step * a_im)
    den = a_re * a_re + a_im * a_im
    p = ab_re - 1.0
    f_re = (p * a_re + ab_im * a_im) / den
    f_im = (ab_im * a_re - p * a_im) / den
    are_ref[...] = ab_re
    aim_ref[...] = ab_im
    bbre_ref[...] = f_re[:, None, :] * bre_ref[...] - f_im[:, None, :] * bim_ref[...]
    bbim_ref[...] = f_re[:, None, :] * bim_ref[...] + f_im[:, None, :] * bre_ref[...]


def _s5prep(lam_re, lam_im, log_step, b_re_t, b_im_t):
    g, n = lam_re.shape
    c = b_re_t.shape[1]
    return pl.pallas_call(
        _s5prep_kernel,
        out_shape=[jax.ShapeDtypeStruct((g, n), F32), jax.ShapeDtypeStruct((g, n), F32),
                   jax.ShapeDtypeStruct((g, c, n), F32), jax.ShapeDtypeStruct((g, c, n), F32)],
        name="s5prep",
    )(lam_re, lam_im, log_step, b_re_t, b_im_t)


S5_COLS = N_GROUPS * SSM_STATE
S5_NT = 256
S5_SCAN_COLS = 512
S5_PARTS = 4


def _s5_kernel(u_ref, bwre_ref, bwim_ref, are_ref, aim_ref, cwre_ref, cwim_ref, d_ref, wglu_ref,
               bglu_ref, gn_ref, o_ref, sre_ref, sim_ref, stre_ref, stim_ref, ut_ref, yt_ref):
    bsz, tt, _ = u_ref.shape
    n_slab = D_SSM // LANES

    @pl.when(pl.program_id(0) == 0)
    def _():
        stre_ref[...] = jnp.zeros_like(stre_ref)
        stim_ref[...] = jnp.zeros_like(stim_ref)

    for b in range(bsz):
        for s in range(n_slab):
            ut_ref[s, pl.ds(b, tt, stride=bsz), :] = u_ref[b, :, s * LANES:(s + 1) * LANES]
    n_parts = S5_PARTS
    tq = tt // n_parts
    n_chunks = S5_COLS // S5_SCAN_COLS
    chunks = [slice(cc * S5_SCAN_COLS, (cc + 1) * S5_SCAN_COLS) for cc in range(n_chunks)]
    a_re = [are_ref[:, cs] for cs in chunks]
    a_im = [aim_ref[:, cs] for cs in chunks]
    state = [(stre_ref[:, cs], stim_ref[:, cs]) for cs in chunks]
    half = S5_COLS // 2

    def part_rows(q):
        return slice(q * tq * bsz, (q + 1) * tq * bsz)

    def bu(q):
        rs = part_rows(q)
        slabs = [ut_ref[s, rs, :].astype(BF16) for s in range(n_slab)]
        for j in range(S5_COLS // S5_NT):
            cs = slice(j * S5_NT, (j + 1) * S5_NT)
            sre_ref[rs, cs] = jnp.dot(slabs[j // 2], bwre_ref[j], preferred_element_type=F32)
            sim_ref[rs, cs] = jnp.dot(slabs[j // 2], bwim_ref[j], preferred_element_type=F32)

    def scan(q):
        for cc, cs in enumerate(chunks):
            s_re, s_im = state[cc]
            for t in range(q * tq, (q + 1) * tq):
                rs = slice(t * bsz, (t + 1) * bsz)
                n_re = a_re[cc] * s_re - a_im[cc] * s_im + sre_ref[rs, cs]
                n_im = a_re[cc] * s_im + a_im[cc] * s_re + sim_ref[rs, cs]
                sre_ref[rs, cs] = n_re
                sim_ref[rs, cs] = n_im
                s_re, s_im = n_re, n_im
            state[cc] = (s_re, s_im)

    y_parts = {}

    def cs_out(q):
        rs = part_rows(q)
        ys = []
        for n in range(2):
            ks = slice(n * half, (n + 1) * half)
            ys.append(jnp.dot(sre_ref[rs, ks].astype(BF16), cwre_ref[n], preferred_element_type=F32)
                      - jnp.dot(sim_ref[rs, ks].astype(BF16), cwim_ref[n], preferred_element_type=F32))
        u = jnp.concatenate([ut_ref[s, rs, :] for s in range(n_slab)], axis=1)
        y_parts[q] = jax.nn.gelu(jnp.concatenate(ys, axis=1) + d_ref[...] * u)

    def glu(q):
        rs = part_rows(q)
        y = y_parts.pop(q)
        z = jnp.dot(y.astype(BF16), wglu_ref[...], preferred_element_type=F32) + bglu_ref[...]
        yn = _rms_rows(y * jax.nn.sigmoid(z), gn_ref[...])
        for s in range(n_slab):
            yt_ref[s, rs, :] = yn[:, s * LANES:(s + 1) * LANES]

    bu(0)
    for q in range(n_parts + 3):
        if q + 1 < n_parts:
            bu(q + 1)
        if q < n_parts:
            scan(q)
        if 0 <= q - 1 < n_parts:
            cs_out(q - 1)
        if 0 <= q - 2 < n_parts:
            glu(q - 2)
    for cc, cs in enumerate(chunks):
        stre_ref[:, cs], stim_ref[:, cs] = state[cc]

    for b in range(bsz):
        for s in range(n_slab):
            o_ref[b, :, s * LANES:(s + 1) * LANES] = yt_ref[s, pl.ds(b, tt, stride=bsz), :].astype(o_ref.dtype)


def _s5(u, bw_re, bw_im, a_re8, a_im8, cw_re, cw_im, d, w_glu, b_glu, gn, tt):
    bsz, t, _ = u.shape
    assert bsz == SUBLANES
    rows = tt * bsz
    full = lambda a: pl.BlockSpec(a.shape, lambda i: (0,) * a.ndim)
    consts = (bw_re, bw_im, a_re8, a_im8, cw_re, cw_im, d, w_glu, b_glu, gn)
    slab_rows = pltpu.VMEM((D_SSM // LANES, rows, LANES), F32)
    return pl.pallas_call(
        _s5_kernel,
        grid=(t // tt,),
        in_specs=[pl.BlockSpec((bsz, tt, D_SSM), lambda i: (0, i, 0))] + [full(a) for a in consts],
        out_specs=pl.BlockSpec((bsz, tt, D_SSM), lambda i: (0, i, 0)),
        out_shape=jax.ShapeDtypeStruct((bsz, t, D_SSM), BF16),
        scratch_shapes=[pltpu.VMEM((rows, S5_COLS), F32), pltpu.VMEM((rows, S5_COLS), F32),
                        pltpu.VMEM((bsz, S5_COLS), F32), pltpu.VMEM((bsz, S5_COLS), F32),
                        slab_rows, slab_rows],
        compiler_params=_cparams(1),
        name="s5",
    )(u, *consts)


def _t5_bucket_table(n):
    d = np.arange(n)
    max_exact = N_BUCKETS // 2
    nf = np.maximum(d, max_exact).astype(np.float32)
    large = max_exact + (np.log(nf / max_exact) / math.log(MAX_DISTANCE / max_exact)
                         * (N_BUCKETS - max_exact)).astype(np.int32)
    large = np.minimum(large, N_BUCKETS - 1)
    return np.where(d < max_exact, d, large).astype(np.int32)


def _bucket_tiles():
    tbl = _t5_bucket_table(2 * MOBA_BLOCK)
    r = np.arange(MOBA_BLOCK)[:, None]
    c = np.arange(MOBA_BLOCK)[None, :]
    d_own = c - r
    own = np.where(d_own >= 0, tbl[np.maximum(d_own, 0)], -1).astype(np.int32)
    prev = tbl[d_own + MOBA_BLOCK].astype(np.int32)
    return own, prev


def _biastile_kernel(rb_ref, own_ref, prev_ref, o_ref):
    h = pl.program_id(0)
    own = own_ref[...]
    prev = prev_ref[...]
    t_own = jnp.full(own.shape, NEG, F32)
    t_prev = jnp.zeros(prev.shape, F32)
    for b in range(N_BUCKETS):
        v = rb_ref[b, h]
        t_own = jnp.where(own == b, v, t_own)
        t_prev = jnp.where(prev == b, v, t_prev)
    o_ref[0, 0] = t_own
    o_ref[0, 1] = t_prev


def _biastile(rel_bias, own, prev):
    blk = own.shape
    return pl.pallas_call(
        _biastile_kernel,
        grid=(N_HEADS,),
        in_specs=[pl.BlockSpec(memory_space=pltpu.SMEM),
                  pl.BlockSpec(blk, lambda h: (0, 0)), pl.BlockSpec(blk, lambda h: (0, 0))],
        out_specs=pl.BlockSpec((1, 2) + blk, lambda h: (h, 0, 0, 0)),
        out_shape=jax.ShapeDtypeStruct((N_HEADS, 2) + blk, F32),
        compiler_params=_cparams(1),
        name="biastile",
    )(rel_bias, own, prev)


MOBA_BIAS_PARTS = 3
MOBA_ONES_ROWS = 16


def _moba_selector(nb):
    e = np.zeros((nb, MOBA_BLOCK, LANES), np.float32)
    for j in range(nb):
        e[j, :, j] = 1.0
        e[j, :, nb:nb + MOBA_BIAS_PARTS] = 1.0
    return e


def _moba_kernel(rb_ref, qt_ref, k_ref, vt_ref, kmean_ref, bias_ref, esel_ref, o_ref,
                 qpad_ref, sel_ref, m_ref, acc_ref):
    i = pl.program_id(1)
    nb = kmean_ref.shape[1]
    tq = qt_ref.shape[3]
    pair_w = 2 * HEAD_DIM
    heads = range(N_HEADS)

    def pair_cols(h):
        return slice((h // 2) * pair_w, (h // 2 + 1) * pair_w)

    blk = lax.broadcasted_iota(jnp.int32, (nb, tq), 0)
    for h in heads:
        qt = qt_ref[0, 0, h * HEAD_DIM:(h + 1) * HEAD_DIM, :]
        zero = jnp.zeros_like(qt)
        qpad = jnp.concatenate([qt, zero] if h % 2 == 0 else [zero, qt], axis=0)
        gate = jnp.dot(kmean_ref[0, :, pair_cols(h)], qpad, preferred_element_type=F32)
        g = jnp.where(blk < i, gate, -jnp.inf)
        sel = jnp.zeros((nb, tq), F32)
        for _ in range(MOBA_TOPK):
            mx = jnp.max(g, axis=0, keepdims=True)
            cand = jnp.where((g == mx) & (mx > -jnp.inf), blk, nb)
            idx = jnp.min(cand, axis=0, keepdims=True)
            pick = blk == idx
            sel = jnp.where(pick, 1.0, sel)
            g = jnp.where(pick, -jnp.inf, g)
        mask = jnp.where(sel > 0.0, 0.0, NEG)
        sel_ref[h] = mask
        c = jnp.full((nb, tq), rb_ref[N_BUCKETS - 1, h], F32)
        cb = jnp.zeros((nb, tq), F32)
        for r in range(MOBA_BIAS_PARTS):
            pc = c.astype(BF16).astype(F32)
            cb = jnp.where(blk == r, pc, cb)
            c = c - pc
        pad = jnp.zeros((pair_w - 2 * nb, tq), BF16)
        qpad_ref[h] = jnp.concatenate([qpad, mask.astype(BF16), cb.astype(BF16), pad], axis=0)

    def key_block(h, j):
        r0 = pl.multiple_of(j * MOBA_BLOCK, MOBA_BLOCK)
        return k_ref[0, pl.ds(r0, MOBA_BLOCK), pair_cols(h)]

    def near_scores(h, j, bias):
        return jnp.dot(key_block(h, j), qpad_ref[h, 0:pair_w], preferred_element_type=F32) + bias

    def far_scores(h, j):
        lhs = jnp.concatenate([key_block(h, j), esel_ref[j]], axis=1)
        return jnp.dot(lhs, qpad_ref[h], preferred_element_type=F32)

    ones = jnp.ones((MOBA_ONES_ROWS, MOBA_BLOCK), BF16)

    def tile_pass(tiles, state):
        queue = {}
        for t in range(min(MOBA_LOOKAHEAD, len(tiles))):
            queue[t] = tiles[t][2]()
        for t in range(len(tiles)):
            if t + MOBA_LOOKAHEAD < len(tiles):
                queue[t + MOBA_LOOKAHEAD] = tiles[t + MOBA_LOOKAHEAD][2]()
            h, j, _ = tiles[t]
            s = queue.pop(t)
            m_new = jnp.max(s, axis=0, keepdims=True)
            if h in state:
                m_new = jnp.maximum(state[h][0], m_new)
            p = jnp.exp(s - m_new).astype(BF16)
            v1 = jnp.concatenate([vt_ref[0, j, h * HEAD_DIM:(h + 1) * HEAD_DIM, :], ones], axis=0)
            acc_new = jnp.dot(v1, p, preferred_element_type=F32)
            if h in state:
                acc_new = jnp.exp(state[h][0] - m_new) * state[h][1] + acc_new
            state[h] = (m_new, acc_new)
        return state

    def load_state():
        return {h: (m_ref[h], acc_ref[h]) for h in heads}

    def store_state(state):
        for h in heads:
            m_ref[h], acc_ref[h] = state[h]

    def own_tiles():
        return [(h, i, functools.partial(near_scores, h, i, bias_ref[h, 0])) for h in heads]

    @pl.when(i == 0)
    def _():
        store_state(tile_pass(own_tiles(), {}))

    def far_tiles(j0, count):
        return [(h, j0 + jj, functools.partial(far_scores, h, j0 + jj)) for jj in range(count) for h in heads]

    n_far = jnp.maximum(i - 1, 0)
    rem = n_far % MOBA_FAR_GROUP
    for r in range(MOBA_FAR_GROUP):
        @pl.when((i >= 1) & (rem == r))
        def _(r=r):
            j = i - 1
            prev = [(h, j, functools.partial(near_scores, h, j, bias_ref[h, 1] + sel_ref[h, pl.ds(j, 1), :]))
                    for h in heads]
            store_state(tile_pass(own_tiles() + prev + far_tiles(0, r), {}))

    def far_group(it, flat):
        state = tile_pass(far_tiles(rem + MOBA_FAR_GROUP * it, MOBA_FAR_GROUP), {h: flat[h] for h in heads})
        return tuple(state[h] for h in heads)

    state0 = load_state()
    final = lax.fori_loop(0, n_far // MOBA_FAR_GROUP, far_group, tuple(state0[h] for h in heads))

    for h in heads:
        acc = final[h][1]
        o_ref[0, 0, h * HEAD_DIM:(h + 1) * HEAD_DIM, :] = (
            acc[:HEAD_DIM] / acc[HEAD_DIM:HEAD_DIM + 1]).astype(o_ref.dtype)


def _moba(rel_bias, qt, k, vt, kmean, bias_tiles):
    bsz, nb, _, tq = qt.shape
    t = k.shape[1]
    esel = jnp.asarray(_moba_selector(nb), BF16)
    once = lambda a: pl.BlockSpec(a.shape, lambda b, i: (0,) * a.ndim, pipeline_mode=pl.Buffered(1))
    return pl.pallas_call(
        _moba_kernel,
        grid=(bsz, nb),
        in_specs=[
            pl.BlockSpec(memory_space=pltpu.SMEM),
            pl.BlockSpec((1, 1, D_ATTN, tq), lambda b, i: (b, i, 0, 0)),
            pl.BlockSpec((1, t, D_ATTN), lambda b, i: (b, 0, 0)),
            pl.BlockSpec((1, nb, D_ATTN, tq), lambda b, i: (b, 0, 0, 0)),
            pl.BlockSpec((1, nb, D_ATTN), lambda b, i: (b, 0, 0)),
            once(bias_tiles), once(esel),
        ],
        out_specs=pl.BlockSpec((1, 1, D_ATTN, tq), lambda b, i: (b, i, 0, 0)),
        out_shape=jax.ShapeDtypeStruct((bsz, nb, D_ATTN, tq), BF16),
        scratch_shapes=[pltpu.VMEM((N_HEADS, 4 * HEAD_DIM, tq), BF16), pltpu.VMEM((N_HEADS, nb, tq), F32),
                        pltpu.VMEM((N_HEADS, 1, tq), F32),
                        pltpu.VMEM((N_HEADS, HEAD_DIM + MOBA_ONES_ROWS, tq), F32)],
        compiler_params=_cparams(2),
        name="moba",
    )(rel_bias, qt, k, vt, kmean, bias_tiles, esel)


def _memkv_kernel(mem_ref, g_ref, wkv_ref, kg_ref, k_ref, v_ref):
    mn = _rms_rows(mem_ref[0], g_ref[...]).astype(BF16)
    kv = jnp.dot(mn, wkv_ref[...], preferred_element_type=F32)
    ks = []
    for hh in range(X_HEADS):
        cs = slice(hh * X_HEAD_DIM, (hh + 1) * X_HEAD_DIM)
        ks.append(_rms_rows(kv[:, cs], kg_ref[...]))
    k_ref[0] = jnp.concatenate(ks, axis=1).astype(BF16)
    v_ref[0] = kv[:, D_X:].astype(BF16)


def _memkv(mem, g, w_kv, kg):
    bsz, m, d = mem.shape
    full = lambda a: pl.BlockSpec(a.shape, lambda b: (0,) * a.ndim)
    return pl.pallas_call(
        _memkv_kernel,
        grid=(bsz,),
        in_specs=[pl.BlockSpec((1, m, d), lambda b: (b, 0, 0)), full(g), full(w_kv), full(kg)],
        out_specs=[pl.BlockSpec((1, m, D_X), lambda b: (b, 0, 0))] * 2,
        out_shape=[jax.ShapeDtypeStruct((bsz, m, D_X), BF16)] * 2,
        compiler_params=_cparams(1),
        name="memkv",
    )(mem, g, w_kv, kg)


OUTX_PIECE = 512


def _outx_kernel(x_ref, ys_ref, yat_ref, ga_ref, wout_ref, gx_ref, wq_ref, qg_ref, kx_ref, vx_ref,
                 wo_ref, o_ref):
    per_piece = OUTX_PIECE // MOBA_BLOCK
    pieces = range(yat_ref.shape[1] // per_piece)
    rows = [slice(r * OUTX_PIECE, (r + 1) * OUTX_PIECE) for r in pieces]
    ya_n = []
    for r in pieces:
        parts = []
        for blk in range(r * per_piece, (r + 1) * per_piece):
            ya = yat_ref[0, blk].astype(F32)
            ms = jnp.mean(ya * ya, axis=0, keepdims=True)
            yan = ya * lax.rsqrt(ms + EPS) * jnp.tile(ga_ref[...], (1, ya.shape[1] // LANES))
            parts.append(yan.T)
        ya_n.append(jnp.concatenate(parts, axis=0).astype(BF16))
    h1 = [x_ref[0, rows[r]]
          + jnp.dot(ys_ref[0, rows[r]], wout_ref[:D_SSM], preferred_element_type=F32)
          + jnp.dot(ya_n[r], wout_ref[D_SSM:], preferred_element_type=F32) for r in pieces]
    hn = [_rms_rows(h1[r], gx_ref[...]).astype(BF16) for r in pieces]
    q = [jnp.dot(hn[r], wq_ref[...], preferred_element_type=F32) for r in pieces]
    outs = [[] for _ in pieces]
    head_cols = [slice(hh * X_HEAD_DIM, (hh + 1) * X_HEAD_DIM) for hh in range(X_HEADS)]

    def scores(hh):
        cs = head_cols[hh]
        qh = [_rms_rows(q[r][:, cs], qg_ref[...]).astype(BF16) for r in pieces]
        return [lax.dot_general(qh[r], kx_ref[0, :, cs], (((1,), (1,)), ((), ())),
                                preferred_element_type=F32) * (X_HEAD_DIM ** -0.5) for r in pieces]

    s_next = scores(0)
    for hh in range(X_HEADS):
        s = s_next
        if hh + 1 < X_HEADS:
            s_next = scores(hh + 1)
        for r in pieces:
            e = jnp.exp(s[r] - jnp.max(s[r], axis=-1, keepdims=True))
            prob = e / jnp.sum(e, axis=-1, keepdims=True)
            outs[r].append(jnp.dot(prob.astype(BF16), vx_ref[0, :, head_cols[hh]],
                                   preferred_element_type=F32))
    for r in pieces:
        o = jnp.concatenate(outs[r], axis=1).astype(BF16)
        o_ref[0, rows[r]] = h1[r] + jnp.dot(o, wo_ref[...], preferred_element_type=F32)


def _outx(x, ys, yat, ga, w_out, gx, w_q, qg, kx, vx, w_o, tm):
    bsz, t, d = x.shape
    m = kx.shape[1]
    rb = tm // MOBA_BLOCK
    full = lambda a: pl.BlockSpec(a.shape, lambda b, i: (0,) * a.ndim)
    return pl.pallas_call(
        _outx_kernel,
        grid=(bsz, t // tm),
        in_specs=[
            pl.BlockSpec((1, tm, d), lambda b, i: (b, i, 0)),
            pl.BlockSpec((1, tm, D_SSM), lambda b, i: (b, i, 0)),
            pl.BlockSpec((1, rb, D_ATTN, MOBA_BLOCK), lambda b, i: (b, i, 0, 0)),
            full(ga), full(w_out), full(gx), full(w_q), full(qg),
            pl.BlockSpec((1, m, D_X), lambda b, i: (b, 0, 0)),
            pl.BlockSpec((1, m, D_X), lambda b, i: (b, 0, 0)),
            full(w_o),
        ],
        out_specs=pl.BlockSpec((1, tm, d), lambda b, i: (b, i, 0)),
        out_shape=jax.ShapeDtypeStruct((bsz, t, d), F32),
        compiler_params=_cparams(2),
        name="outx",
    )(x, ys, yat, ga, w_out, gx, w_q, qg, kx, vx, w_o)


FFN_CHUNK = 1408


def _ffn_kernel(h_ref, g_ref, wup_ref, cw_ref, cb_ref, wdn_ref, o_ref, tail_ref):
    tm = h_ref.shape[1]

    @pl.when(pl.program_id(1) == 0)
    def _():
        tail_ref[...] = jnp.zeros_like(tail_ref)

    h = h_ref[0]
    hn = _rms_rows(h, g_ref[...]).astype(BF16)
    acc = h

    def conv(up, cols):
        ext = jnp.concatenate([tail_ref[:, cols], up], axis=0)
        tail_ref[:, cols] = up[tm - SUBLANES:]
        w = cw_ref[:, cols]
        return (w[0:1] * ext[SUBLANES - 2:SUBLANES - 2 + tm]
                + w[1:2] * ext[SUBLANES - 1:SUBLANES - 1 + tm]
                + w[2:3] * up + cb_ref[:, cols])

    n_chunks = D_FF // FFN_CHUNK
    cols = [(slice(c * FFN_CHUNK, (c + 1) * FFN_CHUNK),
             slice(D_FF + c * FFN_CHUNK, D_FF + (c + 1) * FFN_CHUNK)) for c in range(n_chunks)]
    ups = [tuple(jnp.dot(hn, wup_ref[:, cs], preferred_element_type=F32) for cs in cols[c])
           for c in range(n_chunks)]
    for c in range(n_chunks):
        act = (jax.nn.silu(conv(ups[c][0], cols[c][0])) * conv(ups[c][1], cols[c][1])).astype(BF16)
        acc = acc + jnp.dot(act, wdn_ref[cols[c][0]], preferred_element_type=F32)
    o_ref[0] = acc


def _ffn(h, g, w_up, conv_w, conv_b, w_down, tm):
    bsz, t, d = h.shape
    full = lambda a: pl.BlockSpec(a.shape, lambda b, i: (0,) * a.ndim, pipeline_mode=pl.Buffered(1))
    return pl.pallas_call(
        _ffn_kernel,
        grid=(bsz, t // tm),
        in_specs=[pl.BlockSpec((1, tm, d), lambda b, i: (b, i, 0)),
                  full(g), full(w_up), full(conv_w), full(conv_b), full(w_down)],
        out_specs=pl.BlockSpec((1, tm, d), lambda b, i: (b, i, 0)),
        out_shape=jax.ShapeDtypeStruct((bsz, t, d), F32),
        scratch_shapes=[pltpu.VMEM((SUBLANES, 2 * D_FF), F32)],
        compiler_params=_cparams(2),
        name="ffn",
    )(h, g, w_up, conv_w, conv_b, w_down)


def _block_diag_b(bb):
    g, c, n = bb.shape
    per_tile = S5_NT // n
    per_slab = LANES // c
    tiles = S5_COLS // S5_NT
    place = np.zeros((tiles, per_slab, per_tile), np.float32)
    for j in range(tiles):
        for q in range(per_tile):
            place[j, (j * per_tile + q) % per_slab, q] = 1.0
    out = jnp.einsum('jqcn,jgq->jgcqn', bb.reshape(tiles, per_tile, c, n), jnp.asarray(place))
    return out.reshape(tiles, LANES, S5_NT).astype(BF16)


def _block_diag_c(cm):
    g, c, n = cm.shape
    per_tile = g // 2
    out = jnp.einsum('tqcn,qp->tqnpc', cm.reshape(2, per_tile, c, n), jnp.eye(per_tile, dtype=F32))
    return out.reshape(2, per_tile * n, per_tile * c).astype(BF16)


def kernel(x, mem, norm_mix, w_in, attn_qn, attn_kn, ssm_lam_re, ssm_lam_im, ssm_log_step, ssm_b_re, ssm_b_im, ssm_c_re, ssm_c_im, ssm_d, ssm_w_glu, ssm_b_glu, gnorm_ssm, gnorm_attn, w_out, rel_bias, norm_xattn, norm_mem, x_wq, x_wkv, x_wo, x_qn, x_kn, norm_ffn, ffn_w_up, ffn_conv_w, ffn_conv_b, ffn_w_down):
    bsz, t, d = x.shape
    assert bsz == SUBLANES and t % (2 * max(INPROJ_PIECE, OUTX_PIECE)) == 0
    assert norm_mix.shape[0] == 1
    l = 0
    row = lambda v: v.reshape(1, -1).astype(F32)

    w = w_in[l]
    w_u = w[:, :D_SSM].astype(BF16)
    w_kqvt = jnp.concatenate([w[:, D_SSM + D_ATTN:D_SSM + 2 * D_ATTN], w[:, D_SSM:D_SSM + D_ATTN],
                              w[:, D_SSM + 2 * D_ATTN:]], axis=1).T.astype(BF16)
    lane_rep = lambda v: jnp.broadcast_to(v[:, None], (v.shape[0], LANES)).astype(F32)
    qg = lane_rep(jnp.tile(attn_qn[l], N_HEADS) * (HEAD_DIM ** -0.5))
    kg = lane_rep(jnp.tile(attn_kn[l], N_HEADS))
    u, k, kmean, qt, vt = _inproj(x, row(norm_mix[l]), w_u, w_kqvt, kg, qg, tm=2 * INPROJ_PIECE)

    a_re, a_im, bb_re, bb_im = _s5prep(ssm_lam_re[l], ssm_lam_im[l], ssm_log_step[l][:, None],
                                       jnp.swapaxes(ssm_b_re[l], 1, 2), jnp.swapaxes(ssm_b_im[l], 1, 2))
    a_re8 = jnp.broadcast_to(a_re.reshape(1, -1), (SUBLANES, S5_COLS))
    a_im8 = jnp.broadcast_to(a_im.reshape(1, -1), (SUBLANES, S5_COLS))
    ys = _s5(u, _block_diag_b(bb_re), _block_diag_b(bb_im), a_re8, a_im8,
             _block_diag_c(ssm_c_re[l]), _block_diag_c(ssm_c_im[l]), row(ssm_d[l]),
             ssm_w_glu[l].astype(BF16), row(ssm_b_glu[l]), row(gnorm_ssm[l]), tt=S5_TIME_TILE)

    own, prev = _bucket_tiles()
    bias_tiles = _biastile(rel_bias, jnp.asarray(own), jnp.asarray(prev))
    nb = t // MOBA_BLOCK
    yat = _moba(rel_bias, qt, k, vt, kmean.reshape(bsz, nb, D_ATTN).astype(BF16), bias_tiles)

    kx, vx = _memkv(mem, row(norm_mem[l]), x_wkv[l].astype(BF16), row(x_kn[l]))
    ga = jnp.broadcast_to(gnorm_attn[l][:, None], (D_ATTN, LANES)).astype(F32)
    h2 = _outx(x, ys, yat, ga, w_out[l].astype(BF16), row(norm_xattn[l]), x_wq[l].astype(BF16),
               row(x_qn[l]), kx, vx, x_wo[l].astype(BF16), tm=2 * OUTX_PIECE)

    return _ffn(h2, row(norm_ffn[l]), ffn_w_up[l].astype(BF16), ffn_conv_w[l].reshape(3, 2 * D_FF),
                row(ffn_conv_b[l]), ffn_w_down[l].astype(BF16), tm=FFN_ROWS)
```

```python
import functools
import math

import numpy as np
import jax
import jax.numpy as jnp
from jax import lax
from jax.experimental import pallas as pl
from jax.experimental.pallas import tpu as pltpu

F32 = jnp.float32
BF16 = jnp.bfloat16

EPS = 1e-6
NEG = -1e30
LOG2E = 1.4426950408889634

D_SSM = 512
N_GROUPS = 32
SSM_STATE = 64
D_ATTN = 512
HEAD_DIM = 64
N_HEADS = 8
MOBA_BLOCK = 256
MOBA_TOPK = 3
MOBA_LOOKAHEAD = 6
MOBA_FAR_GROUP = 8
N_BUCKETS = 32
MAX_DISTANCE = 128
X_HEADS = 4
X_HEAD_DIM = 128
D_X = 512
D_FF = 2816
LANES = 128
SUBLANES = 8
V7X_VMEM_BYTES = 64 * 1024 * 1024
VMEM_LIMIT = V7X_VMEM_BYTES * 7 // 8

S5_TIME_TILE = 128
FFN_ROWS = 512


def _cparams(n_axes):
    return pltpu.CompilerParams(
        dimension_semantics=("arbitrary",) * n_axes, vmem_limit_bytes=VMEM_LIMIT)


def _rms_rows(x, g):
    ms = jnp.mean(x * x, axis=-1, keepdims=True)
    return x * lax.rsqrt(ms + EPS) * g


INPROJ_PIECE = 512


def _inproj_kernel(x_ref, g_ref, wu_ref, wkqvt_ref, kg_ref, qg_ref,
                   u_ref, k_ref, kmean_ref, qt_ref, vt_ref):
    tp = INPROJ_PIECE
    pieces = range(x_ref.shape[1] // tp)
    per_piece = tp // MOBA_BLOCK
    rows = [slice(r * tp, (r + 1) * tp) for r in pieces]
    hn = [_rms_rows(x_ref[0, rows[r]], g_ref[...]).astype(BF16) for r in pieces]
    p_u, p_t = [], []
    for r in pieces:
        p_u.append(jnp.dot(hn[r], wu_ref[...], preferred_element_type=F32))
        p_t.append(lax.dot_general(wkqvt_ref[...], hn[r], (((1,), (1,)), ((), ())),
                                   preferred_element_type=F32))

    def head_norm_t(xt, gain_ref):
        x3 = xt.reshape(N_HEADS, HEAD_DIM, tp)
        ms = jnp.mean(x3 * x3, axis=1, keepdims=True)
        return (x3 * lax.rsqrt(ms + EPS)).reshape(D_ATTN, tp) * jnp.tile(gain_ref[...], (1, tp // LANES))

    for r in pieces:
        u_ref[0, rows[r]] = p_u[r]
        kn = head_norm_t(p_t[r][:D_ATTN], kg_ref).T
        k_ref[0, rows[r]] = kn.astype(BF16)
        qn = head_norm_t(p_t[r][D_ATTN:2 * D_ATTN], qg_ref)
        vt = p_t[r][2 * D_ATTN:]
        for b in range(per_piece):
            cs = slice(b * MOBA_BLOCK, (b + 1) * MOBA_BLOCK)
            qt_ref[0, r * per_piece + b] = qn[:, cs].astype(BF16)
            vt_ref[0, r * per_piece + b] = vt[:, cs].astype(BF16)
            kmean_ref[0, r * per_piece + b] = jnp.mean(kn[cs], axis=0, keepdims=True)


def _inproj(x, g, w_u, w_kqvt, kg, qg, tm):
    bsz, t, d = x.shape
    nb = t // MOBA_BLOCK
    rb = tm // MOBA_BLOCK
    full = lambda shp: pl.BlockSpec(shp, lambda b, i: (0,) * len(shp))
    return pl.pallas_call(
        _inproj_kernel,
        grid=(bsz, t // tm),
        in_specs=[
            pl.BlockSpec((1, tm, d), lambda b, i: (b, i, 0)),
            full((1, d)), full(w_u.shape), full(w_kqvt.shape), full(kg.shape), full(qg.shape),
        ],
        out_specs=[
            pl.BlockSpec((1, tm, D_SSM), lambda b, i: (b, i, 0)),
            pl.BlockSpec((1, tm, D_ATTN), lambda b, i: (b, i, 0)),
            pl.BlockSpec((1, rb, 1, D_ATTN), lambda b, i: (b, i, 0, 0)),
            pl.BlockSpec((1, rb, D_ATTN, MOBA_BLOCK), lambda b, i: (b, i, 0, 0)),
            pl.BlockSpec((1, rb, D_ATTN, MOBA_BLOCK), lambda b, i: (b, i, 0, 0)),
        ],
        out_shape=[
            jax.ShapeDtypeStruct((bsz, t, D_SSM), F32),
            jax.ShapeDtypeStruct((bsz, t, D_ATTN), BF16),
            jax.ShapeDtypeStruct((bsz, nb, 1, D_ATTN), F32),
            jax.ShapeDtypeStruct((bsz, nb, D_ATTN, MOBA_BLOCK), BF16),
            jax.ShapeDtypeStruct((bsz, nb, D_ATTN, MOBA_BLOCK), BF16),
        ],
        compiler_params=_cparams(2),
        name="inproj",
    )(x, g, w_u, w_kqvt, kg, qg)


def _s5prep_kernel(lre_ref, lim_ref, ls_ref, bre_ref, bim_ref, are_ref, aim_ref, bbre_ref, bbim_ref):
    a_re = jnp.minimum(lre_ref[...], -1e-4)
    a_im = lim_ref[...]
    step = jnp.exp(ls_ref[...])
    mag = jnp.exp(step * a_re)
    ab_re = mag * jnp.cos(step * a_im)
    ab_im = mag * jnp.sin(step * a_im)
    den = a_re * a_re + a_im * a_im
    p = ab_re - 1.0
    f_re = (p * a_re + ab_im * a_im) / den
    f_im = (ab_im * a_re - p * a_im) / den
    are_ref[...] = ab_re
    aim_ref[...] = ab_im
    bbre_ref[...] = f_re[:, None, :] * bre_ref[...] - f_im[:, None, :] * bim_ref[...]
    bbim_ref[...] = f_re[:, None, :] * bim_ref[...] + f_im[:, None, :] * bre_ref[...]


def _s5prep(lam_re, lam_im, log_step, b_re_t, b_im_t):
    g, n = lam_re.shape
    c = b_re_t.shape[1]
    return pl.pallas_call(
        _s5prep_kernel,
        out_shape=[jax.ShapeDtypeStruct((g, n), F32), jax.ShapeDtypeStruct((g, n), F32),
                   jax.ShapeDtypeStruct((g, c, n), F32), jax.ShapeDtypeStruct((g, c, n), F32)],
        name="s5prep",
    )(lam_re, lam_im, log_step, b_re_t, b_im_t)


S5_COLS = N_GROUPS * SSM_STATE
S5_NT = 256
S5_SCAN_COLS = 512
S5_PARTS = 4


def _s5_kernel(u_ref, bwre_ref, bwim_ref, are_ref, aim_ref, cwre_ref, cwim_ref, d_ref, wglu_ref,
               bglu_ref, gn_ref, o_ref, sre_ref, sim_ref, stre_ref, stim_ref, ut_ref, yt_ref):
    bsz, tt, _ = u_ref.shape
    n_slab = D_SSM // LANES

    @pl.when(pl.program_id(0) == 0)
    def _():
        stre_ref[...] = jnp.zeros_like(stre_ref)
        stim_ref[...] = jnp.zeros_like(stim_ref)

    for b in range(bsz):
        for s in range(n_slab):
            ut_ref[s, pl.ds(b, tt, stride=bsz), :] = u_ref[b, :, s * LANES:(s + 1) * LANES]
    n_parts = S5_PARTS
    tq = tt // n_parts
    n_chunks = S5_COLS // S5_SCAN_COLS
    chunks = [slice(cc * S5_SCAN_COLS, (cc + 1) * S5_SCAN_COLS) for cc in range(n_chunks)]
    a_re = [are_ref[:, cs] for cs in chunks]
    a_im = [aim_ref[:, cs] for cs in chunks]
    state = [(stre_ref[:, cs], stim_ref[:, cs]) for cs in chunks]
    half = S5_COLS // 2

    def part_rows(q):
        return slice(q * tq * bsz, (q + 1) * tq * bsz)

    def bu(q):
        rs = part_rows(q)
        slabs = [ut_ref[s, rs, :].astype(BF16) for s in range(n_slab)]
        for j in range(S5_COLS // S5_NT):
            cs = slice(j * S5_NT, (j + 1) * S5_NT)
            sre_ref[rs, cs] = jnp.dot(slabs[j // 2], bwre_ref[j], preferred_element_type=F32)
            sim_ref[rs, cs] = jnp.dot(slabs[j // 2], bwim_ref[j], preferred_element_type=F32)

    def scan(q):
        for cc, cs in enumerate(chunks):
            s_re, s_im = state[cc]
            for t in range(q * tq, (q + 1) * tq):
                rs = slice(t * bsz, (t + 1) * bsz)
                n_re = a_re[cc] * s_re - a_im[cc] * s_im + sre_ref[rs, cs]
                n_im = a_re[cc] * s_im + a_im[cc] * s_re + sim_ref[rs, cs]
                sre_ref[rs, cs] = n_re
                sim_ref[rs, cs] = n_im
                s_re, s_im = n_re, n_im
            state[cc] = (s_re, s_im)

    y_parts = {}

    def cs_out(q):
        rs = part_rows(q)
        ys = []
        for n in range(2):
            ks = slice(n * half, (n + 1) * half)
            ys.append(jnp.dot(sre_ref[rs, ks].astype(BF16), cwre_ref[n], preferred_element_type=F32)
                      - jnp.dot(sim_ref[rs, ks].astype(BF16), cwim_ref[n], preferred_element_type=F32))
        u = jnp.concatenate([ut_ref[s, rs, :] for s in range(n_slab)], axis=1)
        y_parts[q] = jax.nn.gelu(jnp.concatenate(ys, axis=1) + d_ref[...] * u)

    def glu(q):
        rs = part_rows(q)
        y = y_parts.pop(q)
        z = jnp.dot(y.astype(BF16), wglu_ref[...], preferred_element_type=F32) + bglu_ref[...]
        yn = _rms_rows(y * jax.nn.sigmoid(z), gn_ref[...])
        for s in range(n_slab):
            yt_ref[s, rs, :] = yn[:, s * LANES:(s + 1) * LANES]

    bu(0)
    for q in range(n_parts + 3):
        if q + 1 < n_parts:
            bu(q + 1)
        if q < n_parts:
            scan(q)
        if 0 <= q - 1 < n_parts:
            cs_out(q - 1)
        if 0 <= q - 2 < n_parts:
            glu(q - 2)
    for cc, cs in enumerate(chunks):
        stre_ref[:, cs], stim_ref[:, cs] = state[cc]

    for b in range(bsz):
        for s in range(n_slab):
            o_ref[b, :, s * LANES:(s + 1) * LANES] = yt_ref[s, pl.ds(b, tt, stride=bsz), :].astype(o_ref.dtype)


def _s5(u, bw_re, bw_im, a_re8, a_im8, cw_re, cw_im, d, w_glu, b_glu, gn, tt):
    bsz, t, _ = u.shape
    assert bsz == SUBLANES
    rows = tt * bsz
    full = lambda a: pl.BlockSpec(a.shape, lambda i: (0,) * a.ndim)
    consts = (bw_re, bw_im, a_re8, a_im8, cw_re, cw_im, d, w_glu, b_glu, gn)
    slab_rows = pltpu.VMEM((D_SSM // LANES, rows, LANES), F32)
    return pl.pallas_call(
        _s5_kernel,
        grid=(t // tt,),
        in_specs=[pl.BlockSpec((bsz, tt, D_SSM), lambda i: (0, i, 0))] + [full(a) for a in consts],
        out_specs=pl.BlockSpec((bsz, tt, D_SSM), lambda i: (0, i, 0)),
        out_shape=jax.ShapeDtypeStruct((bsz, t, D_SSM), BF16),
        scratch_shapes=[pltpu.VMEM((rows, S5_COLS), F32), pltpu.VMEM((rows, S5_COLS), F32),
                        pltpu.VMEM((bsz, S5_COLS), F32), pltpu.VMEM((bsz, S5_COLS), F32),
                        slab_rows, slab_rows],
        compiler_params=_cparams(1),
        name="s5",
    )(u, *consts)


def _t5_bucket_table(n):
    d = np.arange(n)
    max_exact = N_BUCKETS // 2
    nf = np.maximum(d, max_exact).astype(np.float32)
    large = max_exact + (np.log(nf / max_exact) / math.log(MAX_DISTANCE / max_exact)
                         * (N_BUCKETS - max_exact)).astype(np.int32)
    large = np.minimum(large, N_BUCKETS - 1)
    return np.where(d < max_exact, d, large).astype(np.int32)


def _bucket_tiles():
    tbl = _t5_bucket_table(2 * MOBA_BLOCK)
    r = np.arange(MOBA_BLOCK)[:, None]
    c = np.arange(MOBA_BLOCK)[None, :]
    d_own = c - r
    own = np.where(d_own >= 0, tbl[np.maximum(d_own, 0)], -1).astype(np.int32)
    prev = tbl[d_own + MOBA_BLOCK].astype(np.int32)
    return own, prev


def _biastile_kernel(rb_ref, own_ref, prev_ref, o_ref):
    h = pl.program_id(0)
    own = own_ref[...]
    prev = prev_ref[...]
    t_own = jnp.full(own.shape, NEG, F32)
    t_prev = jnp.zeros(prev.shape, F32)
    for b in range(N_BUCKETS):
        v = rb_ref[b, h] * LOG2E
        t_own = jnp.where(own == b, v, t_own)
        t_prev = jnp.where(prev == b, v, t_prev)
    o_ref[0, 0] = t_own
    o_ref[0, 1] = t_prev


def _biastile(rel_bias, own, prev):
    blk = own.shape
    return pl.pallas_call(
        _biastile_kernel,
        grid=(N_HEADS,),
        in_specs=[pl.BlockSpec(memory_space=pltpu.SMEM),
                  pl.BlockSpec(blk, lambda h: (0, 0)), pl.BlockSpec(blk, lambda h: (0, 0))],
        out_specs=pl.BlockSpec((1, 2) + blk, lambda h: (h, 0, 0, 0)),
        out_shape=jax.ShapeDtypeStruct((N_HEADS, 2) + blk, F32),
        compiler_params=_cparams(1),
        name="biastile",
    )(rel_bias, own, prev)


MOBA_BIAS_PARTS = 3
MOBA_ONES_ROWS = 16


def _moba_selector(nb):
    e = np.zeros((nb, MOBA_BLOCK, LANES), np.float32)
    for j in range(nb):
        e[j, :, j] = 1.0
        e[j, :, nb:nb + MOBA_BIAS_PARTS] = 1.0
    return e


def _moba_kernel(rb_ref, qt_ref, k_ref, vt_ref, kmean_ref, bias_ref, esel_ref, o_ref,
                 qpad_ref, sel_ref, m_ref, acc_ref):
    i = pl.program_id(1)
    nb = kmean_ref.shape[1]
    tq = qt_ref.shape[3]
    pair_w = 2 * HEAD_DIM
    heads = range(N_HEADS)

    def pair_cols(h):
        return slice((h // 2) * pair_w, (h // 2 + 1) * pair_w)

    blk = lax.broadcasted_iota(jnp.int32, (nb, tq), 0)
    for h in heads:
        qt = qt_ref[0, 0, h * HEAD_DIM:(h + 1) * HEAD_DIM, :]
        zero = jnp.zeros_like(qt)
        qpad = jnp.concatenate([qt, zero] if h % 2 == 0 else [zero, qt], axis=0)
        gate = jnp.dot(kmean_ref[0, :, pair_cols(h)], qpad, preferred_element_type=F32)
        g = jnp.where(blk < i, gate, -jnp.inf)
        sel = jnp.zeros((nb, tq), F32)
        for _ in range(MOBA_TOPK):
            mx = jnp.max(g, axis=0, keepdims=True)
            cand = jnp.where((g == mx) & (mx > -jnp.inf), blk, nb)
            idx = jnp.min(cand, axis=0, keepdims=True)
            pick = blk == idx
            sel = jnp.where(pick, 1.0, sel)
            g = jnp.where(pick, -jnp.inf, g)
        mask = jnp.where(sel > 0.0, 0.0, NEG)
        sel_ref[h] = mask
        c = jnp.full((nb, tq), rb_ref[N_BUCKETS - 1, h] * LOG2E, F32)
        cb = jnp.zeros((nb, tq), F32)
        for r in range(MOBA_BIAS_PARTS):
            pc = c.astype(BF16).astype(F32)
            cb = jnp.where(blk == r, pc, cb)
            c = c - pc
        pad = jnp.zeros((pair_w - 2 * nb, tq), BF16)
        qpad_ref[h] = jnp.concatenate([qpad, mask.astype(BF16), cb.astype(BF16), pad], axis=0)

    def key_block(h, j):
        r0 = pl.multiple_of(j * MOBA_BLOCK, MOBA_BLOCK)
        return k_ref[0, pl.ds(r0, MOBA_BLOCK), pair_cols(h)]

    def near_scores(h, j, bias):
        return jnp.dot(key_block(h, j), qpad_ref[h, 0:pair_w], preferred_element_type=F32) + bias

    def far_scores(h, j):
        lhs = jnp.concatenate([key_block(h, j), esel_ref[j]], axis=1)
        return jnp.dot(lhs, qpad_ref[h], preferred_element_type=F32)

    ones = jnp.ones((MOBA_ONES_ROWS, MOBA_BLOCK), BF16)

    def tile_pass(tiles, state):
        queue = {}
        for t in range(min(MOBA_LOOKAHEAD, len(tiles))):
            queue[t] = tiles[t][2]()
        for t in range(len(tiles)):
            if t + MOBA_LOOKAHEAD < len(tiles):
                queue[t + MOBA_LOOKAHEAD] = tiles[t + MOBA_LOOKAHEAD][2]()
            h, j, _ = tiles[t]
            s = queue.pop(t)
            m_new = jnp.max(s, axis=0, keepdims=True)
            if h in state:
                m_new = jnp.maximum(state[h][0], m_new)
            p = jnp.exp2(s - m_new).astype(BF16)
            v1 = jnp.concatenate([vt_ref[0, j, h * HEAD_DIM:(h + 1) * HEAD_DIM, :], ones], axis=0)
            acc_new = jnp.dot(v1, p, preferred_element_type=F32)
            if h in state:
                acc_new = jnp.exp2(state[h][0] - m_new) * state[h][1] + acc_new
            state[h] = (m_new, acc_new)
        return state

    def load_state():
        return {h: (m_ref[h], acc_ref[h]) for h in heads}

    def store_state(state):
        for h in heads:
            m_ref[h], acc_ref[h] = state[h]

    def own_tiles():
        return [(h, i, functools.partial(near_scores, h, i, bias_ref[h, 0])) for h in heads]

    @pl.when(i == 0)
    def _():
        store_state(tile_pass(own_tiles(), {}))

    def far_tiles(j0, count):
        return [(h, j0 + jj, functools.partial(far_scores, h, j0 + jj)) for jj in range(count) for h in heads]

    n_far = jnp.maximum(i - 1, 0)
    rem = n_far % MOBA_FAR_GROUP
    for r in range(MOBA_FAR_GROUP):
        @pl.when((i >= 1) & (rem == r))
        def _(r=r):
            j = i - 1
            prev = [(h, j, functools.partial(near_scores, h, j, bias_ref[h, 1] + sel_ref[h, pl.ds(j, 1), :]))
                    for h in heads]
            store_state(tile_pass(own_tiles() + prev + far_tiles(0, r), {}))

    def far_group(it, flat):
        state = tile_pass(far_tiles(rem + MOBA_FAR_GROUP * it, MOBA_FAR_GROUP), {h: flat[h] for h in heads})
        return tuple(state[h] for h in heads)

    state0 = load_state()
    final = lax.fori_loop(0, n_far // MOBA_FAR_GROUP, far_group, tuple(state0[h] for h in heads))

    for h in heads:
        acc = final[h][1]
        o_ref[0, 0, h * HEAD_DIM:(h + 1) * HEAD_DIM, :] = (
            acc[:HEAD_DIM] / acc[HEAD_DIM:HEAD_DIM + 1]).astype(o_ref.dtype)


def _moba(rel_bias, qt, k, vt, kmean, bias_tiles):
    bsz, nb, _, tq = qt.shape
    t = k.shape[1]
    esel = jnp.asarray(_moba_selector(nb), BF16)
    once = lambda a: pl.BlockSpec(a.shape, lambda b, i: (0,) * a.ndim, pipeline_mode=pl.Buffered(1))
    return pl.pallas_call(
        _moba_kernel,
        grid=(bsz, nb),
        in_specs=[
            pl.BlockSpec(memory_space=pltpu.SMEM),
            pl.BlockSpec((1, 1, D_ATTN, tq), lambda b, i: (b, i, 0, 0)),
            pl.BlockSpec((1, t, D_ATTN), lambda b, i: (b, 0, 0)),
            pl.BlockSpec((1, nb, D_ATTN, tq), lambda b, i: (b, 0, 0, 0)),
            pl.BlockSpec((1, nb, D_ATTN), lambda b, i: (b, 0, 0)),
            once(bias_tiles), once(esel),
        ],
        out_specs=pl.BlockSpec((1, 1, D_ATTN, tq), lambda b, i: (b, i, 0, 0)),
        out_shape=jax.ShapeDtypeStruct((bsz, nb, D_ATTN, tq), BF16),
        scratch_shapes=[pltpu.VMEM((N_HEADS, 4 * HEAD_DIM, tq), BF16), pltpu.VMEM((N_HEADS, nb, tq), F32),
                        pltpu.VMEM((N_HEADS, 1, tq), F32),
                        pltpu.VMEM((N_HEADS, HEAD_DIM + MOBA_ONES_ROWS, tq), F32)],
        compiler_params=_cparams(2),
        name="moba",
    )(rel_bias, qt, k, vt, kmean, bias_tiles, esel)


def _memkv_kernel(mem_ref, g_ref, wkv_ref, kg_ref, k_ref, v_ref):
    mn = _rms_rows(mem_ref[0], g_ref[...]).astype(BF16)
    kv = jnp.dot(mn, wkv_ref[...], preferred_element_type=F32)
    ks = []
    for hh in range(X_HEADS):
        cs = slice(hh * X_HEAD_DIM, (hh + 1) * X_HEAD_DIM)
        ks.append(_rms_rows(kv[:, cs], kg_ref[...]))
    k_ref[0] = jnp.concatenate(ks, axis=1).astype(BF16)
    v_ref[0] = kv[:, D_X:].astype(BF16)


def _memkv(mem, g, w_kv, kg):
    bsz, m, d = mem.shape
    full = lambda a: pl.BlockSpec(a.shape, lambda b: (0,) * a.ndim)
    return pl.pallas_call(
        _memkv_kernel,
        grid=(bsz,),
        in_specs=[pl.BlockSpec((1, m, d), lambda b: (b, 0, 0)), full(g), full(w_kv), full(kg)],
        out_specs=[pl.BlockSpec((1, m, D_X), lambda b: (b, 0, 0))] * 2,
        out_shape=[jax.ShapeDtypeStruct((bsz, m, D_X), BF16)] * 2,
        compiler_params=_cparams(1),
        name="memkv",
    )(mem, g, w_kv, kg)


OUTX_PIECE = 512


def _outx_kernel(x_ref, ys_ref, yat_ref, ga_ref, wout_ref, gx_ref, wq_ref, qg_ref, kx_ref, vx_ref,
                 wo_ref, o_ref):
    per_piece = OUTX_PIECE // MOBA_BLOCK
    pieces = range(yat_ref.shape[1] // per_piece)
    rows = [slice(r * OUTX_PIECE, (r + 1) * OUTX_PIECE) for r in pieces]
    ya_n = []
    for r in pieces:
        parts = []
        for blk in range(r * per_piece, (r + 1) * per_piece):
            ya = yat_ref[0, blk].astype(F32)
            ms = jnp.mean(ya * ya, axis=0, keepdims=True)
            yan = ya * lax.rsqrt(ms + EPS) * jnp.tile(ga_ref[...], (1, ya.shape[1] // LANES))
            parts.append(yan.T)
        ya_n.append(jnp.concatenate(parts, axis=0).astype(BF16))
    h1 = [x_ref[0, rows[r]]
          + jnp.dot(ys_ref[0, rows[r]], wout_ref[:D_SSM], preferred_element_type=F32)
          + jnp.dot(ya_n[r], wout_ref[D_SSM:], preferred_element_type=F32) for r in pieces]
    hn = [_rms_rows(h1[r], gx_ref[...]).astype(BF16) for r in pieces]
    q = [jnp.dot(hn[r], wq_ref[...], preferred_element_type=F32) for r in pieces]
    outs = [[] for _ in pieces]
    head_cols = [slice(hh * X_HEAD_DIM, (hh + 1) * X_HEAD_DIM) for hh in range(X_HEADS)]

    def scores(hh):
        cs = head_cols[hh]
        qh = [_rms_rows(q[r][:, cs], qg_ref[...]).astype(BF16) for r in pieces]
        return [lax.dot_general(qh[r], kx_ref[0, :, cs], (((1,), (1,)), ((), ())),
                                preferred_element_type=F32) * (X_HEAD_DIM ** -0.5) for r in pieces]

    s_next = scores(0)
    for hh in range(X_HEADS):
        s = s_next
        if hh + 1 < X_HEADS:
            s_next = scores(hh + 1)
        for r in pieces:
            e = jnp.exp(s[r] - jnp.max(s[r], axis=-1, keepdims=True))
            prob = e / jnp.sum(e, axis=-1, keepdims=True)
            outs[r].append(jnp.dot(prob.astype(BF16), vx_ref[0, :, head_cols[hh]],
                                   preferred_element_type=F32))
    for r in pieces:
        o = jnp.concatenate(outs[r], axis=1).astype(BF16)
        o_ref[0, rows[r]] = h1[r] + jnp.dot(o, wo_ref[...], preferred_element_type=F32)


def _outx(x, ys, yat, ga, w_out, gx, w_q, qg, kx, vx, w_o, tm):
    bsz, t, d = x.shape
    m = kx.shape[1]
    rb = tm // MOBA_BLOCK
    full = lambda a: pl.BlockSpec(a.shape, lambda b, i: (0,) * a.ndim)
    return pl.pallas_call(
        _outx_kernel,
        grid=(bsz, t // tm),
        in_specs=[
            pl.BlockSpec((1, tm, d), lambda b, i: (b, i, 0)),
            pl.BlockSpec((1, tm, D_SSM), lambda b, i: (b, i, 0)),
            pl.BlockSpec((1, rb, D_ATTN, MOBA_BLOCK), lambda b, i: (b, i, 0, 0)),
            full(ga), full(w_out), full(gx), full(w_q), full(qg),
            pl.BlockSpec((1, m, D_X), lambda b, i: (b, 0, 0)),
            pl.BlockSpec((1, m, D_X), lambda b, i: (b, 0, 0)),
            full(w_o),
        ],
        out_specs=pl.BlockSpec((1, tm, d), lambda b, i: (b, i, 0)),
        out_shape=jax.ShapeDtypeStruct((bsz, t, d), F32),
        compiler_params=_cparams(2),
        name="outx",
    )(x, ys, yat, ga, w_out, gx, w_q, qg, kx, vx, w_o)


FFN_CHUNK = 1408


def _ffn_kernel(h_ref, g_ref, wup_ref, cw_ref, cb_ref, wdn_ref, o_ref, tail_ref):
    tm = h_ref.shape[1]

    @pl.when(pl.program_id(1) == 0)
    def _():
        tail_ref[...] = jnp.zeros_like(tail_ref)

    h = h_ref[0]
    hn = _rms_rows(h, g_ref[...]).astype(BF16)
    acc = h

    def conv(up, cols):
        ext = jnp.concatenate([tail_ref[:, cols], up], axis=0)
        tail_ref[:, cols] = up[tm - SUBLANES:]
        w = cw_ref[:, cols]
        return (w[0:1] * ext[SUBLANES - 2:SUBLANES - 2 + tm]
                + w[1:2] * ext[SUBLANES - 1:SUBLANES - 1 + tm]
                + w[2:3] * up + cb_ref[:, cols])

    n_chunks = D_FF // FFN_CHUNK
    cols = [(slice(c * FFN_CHUNK, (c + 1) * FFN_CHUNK),
             slice(D_FF + c * FFN_CHUNK, D_FF + (c + 1) * FFN_CHUNK)) for c in range(n_chunks)]
    ups = [tuple(jnp.dot(hn, wup_ref[:, cs], preferred_element_type=F32) for cs in cols[c])
           for c in range(n_chunks)]
    for c in range(n_chunks):
        act = (jax.nn.silu(conv(ups[c][0], cols[c][0])) * conv(ups[c][1], cols[c][1])).astype(BF16)
        acc = acc + jnp.dot(act, wdn_ref[cols[c][0]], preferred_element_type=F32)
    o_ref[0] = acc


def _ffn(h, g, w_up, conv_w, conv_b, w_down, tm):
    bsz, t, d = h.shape
    full = lambda a: pl.BlockSpec(a.shape, lambda b, i: (0,) * a.ndim, pipeline_mode=pl.Buffered(1))
    return pl.pallas_call(
        _ffn_kernel,
        grid=(bsz, t // tm),
        in_specs=[pl.BlockSpec((1, tm, d), lambda b, i: (b, i, 0)),
                  full(g), full(w_up), full(conv_w), full(conv_b), full(w_down)],
        out_specs=pl.BlockSpec((1, tm, d), lambda b, i: (b, i, 0)),
        out_shape=jax.ShapeDtypeStruct((bsz, t, d), F32),
        scratch_shapes=[pltpu.VMEM((SUBLANES, 2 * D_FF), F32)],
        compiler_params=_cparams(2),
        name="ffn",
    )(h, g, w_up, conv_w, conv_b, w_down)


def _block_diag_b(bb):
    g, c, n = bb.shape
    per_tile = S5_NT // n
    per_slab = LANES // c
    tiles = S5_COLS // S5_NT
    place = np.zeros((tiles, per_slab, per_tile), np.float32)
    for j in range(tiles):
        for q in range(per_tile):
            place[j, (j * per_tile + q) % per_slab, q] = 1.0
    out = jnp.einsum('jqcn,jgq->jgcqn', bb.reshape(tiles, per_tile, c, n), jnp.asarray(place))
    return out.reshape(tiles, LANES, S5_NT).astype(BF16)


def _block_diag_c(cm):
    g, c, n = cm.shape
    per_tile = g // 2
    out = jnp.einsum('tqcn,qp->tqnpc', cm.reshape(2, per_tile, c, n), jnp.eye(per_tile, dtype=F32))
    return out.reshape(2, per_tile * n, per_tile * c).astype(BF16)


def kernel(x, mem, norm_mix, w_in, attn_qn, attn_kn, ssm_lam_re, ssm_lam_im, ssm_log_step, ssm_b_re, ssm_b_im, ssm_c_re, ssm_c_im, ssm_d, ssm_w_glu, ssm_b_glu, gnorm_ssm, gnorm_attn, w_out, rel_bias, norm_xattn, norm_mem, x_wq, x_wkv, x_wo, x_qn, x_kn, norm_ffn, ffn_w_up, ffn_conv_w, ffn_conv_b, ffn_w_down):
    bsz, t, d = x.shape
    assert bsz == SUBLANES and t % (2 * max(INPROJ_PIECE, OUTX_PIECE)) == 0
    assert norm_mix.shape[0] == 1
    l = 0
    row = lambda v: v.reshape(1, -1).astype(F32)

    w = w_in[l]
    w_u = w[:, :D_SSM].astype(BF16)
    w_kqvt = jnp.concatenate([w[:, D_SSM + D_ATTN:D_SSM + 2 * D_ATTN], w[:, D_SSM:D_SSM + D_ATTN],
                              w[:, D_SSM + 2 * D_ATTN:]], axis=1).T.astype(BF16)
    lane_rep = lambda v: jnp.broadcast_to(v[:, None], (v.shape[0], LANES)).astype(F32)
    qg = lane_rep(jnp.tile(attn_qn[l], N_HEADS) * (HEAD_DIM ** -0.5 * LOG2E))
    kg = lane_rep(jnp.tile(attn_kn[l], N_HEADS))
    u, k, kmean, qt, vt = _inproj(x, row(norm_mix[l]), w_u, w_kqvt, kg, qg, tm=2 * INPROJ_PIECE)

    a_re, a_im, bb_re, bb_im = _s5prep(ssm_lam_re[l], ssm_lam_im[l], ssm_log_step[l][:, None],
                                       jnp.swapaxes(ssm_b_re[l], 1, 2), jnp.swapaxes(ssm_b_im[l], 1, 2))
    a_re8 = jnp.broadcast_to(a_re.reshape(1, -1), (SUBLANES, S5_COLS))
    a_im8 = jnp.broadcast_to(a_im.reshape(1, -1), (SUBLANES, S5_COLS))
    ys = _s5(u, _block_diag_b(bb_re), _block_diag_b(bb_im), a_re8, a_im8,
             _block_diag_c(ssm_c_re[l]), _block_diag_c(ssm_c_im[l]), row(ssm_d[l]),
             ssm_w_glu[l].astype(BF16), row(ssm_b_glu[l]), row(gnorm_ssm[l]), tt=S5_TIME_TILE)

    own, prev = _bucket_tiles()
    bias_tiles = _biastile(rel_bias, jnp.asarray(own), jnp.asarray(prev))
    nb = t // MOBA_BLOCK
    yat = _moba(rel_bias, qt, k, vt, kmean.reshape(bsz, nb, D_ATTN).astype(BF16), bias_tiles)

    kx, vx = _memkv(mem, row(norm_mem[l]), x_wkv[l].astype(BF16), row(x_kn[l]))
    ga = jnp.broadcast_to(gnorm_attn[l][:, None], (D_ATTN, LANES)).astype(F32)
    h2 = _outx(x, ys, yat, ga, w_out[l].astype(BF16), row(norm_xattn[l]), x_wq[l].astype(BF16),
               row(x_qn[l]), kx, vx, x_wo[l].astype(BF16), tm=2 * OUTX_PIECE)

    return _ffn(h2, row(norm_ffn[l]), ffn_w_up[l].astype(BF16), ffn_conv_w[l].reshape(3, 2 * D_FF),
                row(ffn_conv_b[l]), ffn_w_down[l].astype(BF16), tm=FFN_ROWS)
```

```python
import functools
import math

import numpy as np
import jax
import jax.numpy as jnp
from jax import lax
from jax.experimental import pallas as pl
from jax.experimental.pallas import tpu as pltpu

F32 = jnp.float32
BF16 = jnp.bfloat16

EPS = 1e-6
NEG = -1e30
LOG2E = 1.4426950408889634

D_SSM = 512
N_GROUPS = 32
SSM_STATE = 64
D_ATTN = 512
HEAD_DIM = 64
N_HEADS = 8
MOBA_BLOCK = 256
MOBA_TOPK = 3
MOBA_LOOKAHEAD = 6
MOBA_FAR_GROUP = 8
N_BUCKETS = 32
MAX_DISTANCE = 128
X_HEADS = 4
X_HEAD_DIM = 128
D_X = 512
D_FF = 2816
LANES = 128
SUBLANES = 8
V7X_VMEM_BYTES = 64 * 1024 * 1024
VMEM_LIMIT = V7X_VMEM_BYTES * 7 // 8

S5_TIME_TILE = 128
FFN_ROWS = 512


def _cparams(n_axes):
    return pltpu.CompilerParams(
        dimension_semantics=("arbitrary",) * n_axes, vmem_limit_bytes=VMEM_LIMIT)


def _rms_rows(x, g):
    ms = jnp.mean(x * x, axis=-1, keepdims=True)
    return x * lax.rsqrt(ms + EPS) * g


INPROJ_PIECE = 512


def _inproj_kernel(x_ref, g_ref, wu_ref, wkqvt_ref, kg_ref, qg_ref,
                   u_ref, k_ref, kmean_ref, qt_ref, vt_ref):
    tp = INPROJ_PIECE
    pieces = range(x_ref.shape[1] // tp)
    per_piece = tp // MOBA_BLOCK
    rows = [slice(r * tp, (r + 1) * tp) for r in pieces]
    hn = [_rms_rows(x_ref[0, rows[r]], g_ref[...]).astype(BF16) for r in pieces]
    p_u, p_t = [], []
    for r in pieces:
        p_u.append(jnp.dot(hn[r], wu_ref[...], preferred_element_type=F32))
        p_t.append(lax.dot_general(wkqvt_ref[...], hn[r], (((1,), (1,)), ((), ())),
                                   preferred_element_type=F32))

    def head_norm_t(xt, gain_ref):
        x3 = xt.reshape(N_HEADS, HEAD_DIM, tp)
        ms = jnp.mean(x3 * x3, axis=1, keepdims=True)
        return (x3 * lax.rsqrt(ms + EPS)).reshape(D_ATTN, tp) * jnp.tile(gain_ref[...], (1, tp // LANES))

    for r in pieces:
        u_ref[0, rows[r]] = p_u[r]
        kn = head_norm_t(p_t[r][:D_ATTN], kg_ref).T
        k_ref[0, rows[r]] = (kn * LOG2E).astype(BF16)
        qn = head_norm_t(p_t[r][D_ATTN:2 * D_ATTN], qg_ref)
        vt = p_t[r][2 * D_ATTN:]
        for b in range(per_piece):
            cs = slice(b * MOBA_BLOCK, (b + 1) * MOBA_BLOCK)
            qt_ref[0, r * per_piece + b] = qn[:, cs].astype(BF16)
            vt_ref[0, r * per_piece + b] = vt[:, cs].astype(BF16)
            kmean_ref[0, r * per_piece + b] = jnp.mean(kn[cs], axis=0, keepdims=True)


def _inproj(x, g, w_u, w_kqvt, kg, qg, tm):
    bsz, t, d = x.shape
    nb = t // MOBA_BLOCK
    rb = tm // MOBA_BLOCK
    full = lambda shp: pl.BlockSpec(shp, lambda b, i: (0,) * len(shp))
    return pl.pallas_call(
        _inproj_kernel,
        grid=(bsz, t // tm),
        in_specs=[
            pl.BlockSpec((1, tm, d), lambda b, i: (b, i, 0)),
            full((1, d)), full(w_u.shape), full(w_kqvt.shape), full(kg.shape), full(qg.shape),
        ],
        out_specs=[
            pl.BlockSpec((1, tm, D_SSM), lambda b, i: (b, i, 0)),
            pl.BlockSpec((1, tm, D_ATTN), lambda b, i: (b, i, 0)),
            pl.BlockSpec((1, rb, 1, D_ATTN), lambda b, i: (b, i, 0, 0)),
            pl.BlockSpec((1, rb, D_ATTN, MOBA_BLOCK), lambda b, i: (b, i, 0, 0)),
            pl.BlockSpec((1, rb, D_ATTN, MOBA_BLOCK), lambda b, i: (b, i, 0, 0)),
        ],
        out_shape=[
            jax.ShapeDtypeStruct((bsz, t, D_SSM), F32),
            jax.ShapeDtypeStruct((bsz, t, D_ATTN), BF16),
            jax.ShapeDtypeStruct((bsz, nb, 1, D_ATTN), F32),
            jax.ShapeDtypeStruct((bsz, nb, D_ATTN, MOBA_BLOCK), BF16),
            jax.ShapeDtypeStruct((bsz, nb, D_ATTN, MOBA_BLOCK), BF16),
        ],
        compiler_params=_cparams(2),
        name="inproj",
    )(x, g, w_u, w_kqvt, kg, qg)


def _s5prep_kernel(lre_ref, lim_ref, ls_ref, bre_ref, bim_ref, are_ref, aim_ref, bbre_ref, bbim_ref):
    a_re = jnp.minimum(lre_ref[...], -1e-4)
    a_im = lim_ref[...]
    step = jnp.exp(ls_ref[...])
    mag = jnp.exp(step * a_re)
    ab_re = mag * jnp.cos(step * a_im)
    ab_im = mag * jnp.sin(step * a_im)
    den = a_re * a_re + a_im * a_im
    p = ab_re - 1.0
    f_re = (p * a_re + ab_im * a_im) / den
    f_im = (ab_im * a_re - p * a_im) / den
    are_ref[...] = ab_re
    aim_ref[...] = ab_im
    bbre_ref[...] = f_re[:, None, :] * bre_ref[...] - f_im[:, None, :] * bim_ref[...]
    bbim_ref[...] = f_re[:, None, :] * bim_ref[...] + f_im[:, None, :] * bre_ref[...]


def _s5prep(lam_re, lam_im, log_step, b_re_t, b_im_t):
    g, n = lam_re.shape
    c = b_re_t.shape[1]
    return pl.pallas_call(
        _s5prep_kernel,
        out_shape=[jax.ShapeDtypeStruct((g, n), F32), jax.ShapeDtypeStruct((g, n), F32),
                   jax.ShapeDtypeStruct((g, c, n), F32), jax.ShapeDtypeStruct((g, c, n), F32)],
        name="s5prep",
    )(lam_re, lam_im, log_step, b_re_t, b_im_t)


S5_COLS = N_GROUPS * SSM_STATE
S5_NT = 256
S5_SCAN_COLS = 512
S5_PARTS = 4


def _s5_kernel(u_ref, bwre_ref, bwim_ref, are_ref, aim_ref, cwre_ref, cwim_ref, d_ref, wglu_ref,
               bglu_ref, gn_ref, o_ref, sre_ref, sim_ref, stre_ref, stim_ref, ut_ref, yt_ref):
    bsz, tt, _ = u_ref.shape
    n_slab = D_SSM // LANES

    @pl.when(pl.program_id(0) == 0)
    def _():
        stre_ref[...] = jnp.zeros_like(stre_ref)
        stim_ref[...] = jnp.zeros_like(stim_ref)

    for b in range(bsz):
        for s in range(n_slab):
            ut_ref[s, pl.ds(b, tt, stride=bsz), :] = u_ref[b, :, s * LANES:(s + 1) * LANES]
    n_parts = S5_PARTS
    tq = tt // n_parts
    n_chunks = S5_COLS // S5_SCAN_COLS
    chunks = [slice(cc * S5_SCAN_COLS, (cc + 1) * S5_SCAN_COLS) for cc in range(n_chunks)]
    a_re = [are_ref[:, cs] for cs in chunks]
    a_im = [aim_ref[:, cs] for cs in chunks]
    state = [(stre_ref[:, cs], stim_ref[:, cs]) for cs in chunks]
    half = S5_COLS // 2

    def part_rows(q):
        return slice(q * tq * bsz, (q + 1) * tq * bsz)

    def bu(q):
        rs = part_rows(q)
        slabs = [ut_ref[s, rs, :].astype(BF16) for s in range(n_slab)]
        for j in range(S5_COLS // S5_NT):
            cs = slice(j * S5_NT, (j + 1) * S5_NT)
            sre_ref[rs, cs] = jnp.dot(slabs[j // 2], bwre_ref[j], preferred_element_type=F32)
            sim_ref[rs, cs] = jnp.dot(slabs[j // 2], bwim_ref[j], preferred_element_type=F32)

    def scan(q):
        for cc, cs in enumerate(chunks):
            s_re, s_im = state[cc]
            for t in range(q * tq, (q + 1) * tq):
                rs = slice(t * bsz, (t + 1) * bsz)
                n_re = a_re[cc] * s_re - a_im[cc] * s_im + sre_ref[rs, cs]
                n_im = a_re[cc] * s_im + a_im[cc] * s_re + sim_ref[rs, cs]
                sre_ref[rs, cs] = n_re
                sim_ref[rs, cs] = n_im
                s_re, s_im = n_re, n_im
            state[cc] = (s_re, s_im)

    y_parts = {}

    def cs_out(q):
        rs = part_rows(q)
        ys = []
        for n in range(2):
            ks = slice(n * half, (n + 1) * half)
            ys.append(jnp.dot(sre_ref[rs, ks].astype(BF16), cwre_ref[n], preferred_element_type=F32)
                      - jnp.dot(sim_ref[rs, ks].astype(BF16), cwim_ref[n], preferred_element_type=F32))
        u = jnp.concatenate([ut_ref[s, rs, :] for s in range(n_slab)], axis=1)
        y_parts[q] = jax.nn.gelu(jnp.concatenate(ys, axis=1) + d_ref[...] * u)

    def glu(q):
        rs = part_rows(q)
        y = y_parts.pop(q)
        z = jnp.dot(y.astype(BF16), wglu_ref[...], preferred_element_type=F32) + bglu_ref[...]
        yn = _rms_rows(y * jax.nn.sigmoid(z), gn_ref[...])
        for s in range(n_slab):
            yt_ref[s, rs, :] = yn[:, s * LANES:(s + 1) * LANES]

    bu(0)
    for q in range(n_parts + 3):
        if q + 1 < n_parts:
            bu(q + 1)
        if q < n_parts:
            scan(q)
        if 0 <= q - 1 < n_parts:
            cs_out(q - 1)
        if 0 <= q - 2 < n_parts:
            glu(q - 2)
    for cc, cs in enumerate(chunks):
        stre_ref[:, cs], stim_ref[:, cs] = state[cc]

    for b in range(bsz):
        for s in range(n_slab):
            o_ref[b, :, s * LANES:(s + 1) * LANES] = yt_ref[s, pl.ds(b, tt, stride=bsz), :].astype(o_ref.dtype)


def _s5(u, bw_re, bw_im, a_re8, a_im8, cw_re, cw_im, d, w_glu, b_glu, gn, tt):
    bsz, t, _ = u.shape
    assert bsz == SUBLANES
    rows = tt * bsz
    full = lambda a: pl.BlockSpec(a.shape, lambda i: (0,) * a.ndim)
    consts = (bw_re, bw_im, a_re8, a_im8, cw_re, cw_im, d, w_glu, b_glu, gn)
    slab_rows = pltpu.VMEM((D_SSM // LANES, rows, LANES), F32)
    return pl.pallas_call(
        _s5_kernel,
        grid=(t // tt,),
        in_specs=[pl.BlockSpec((bsz, tt, D_SSM), lambda i: (0, i, 0))] + [full(a) for a in consts],
        out_specs=pl.BlockSpec((bsz, tt, D_SSM), lambda i: (0, i, 0)),
        out_shape=jax.ShapeDtypeStruct((bsz, t, D_SSM), BF16),
        scratch_shapes=[pltpu.VMEM((rows, S5_COLS), F32), pltpu.VMEM((rows, S5_COLS), F32),
                        pltpu.VMEM((bsz, S5_COLS), F32), pltpu.VMEM((bsz, S5_COLS), F32),
                        slab_rows, slab_rows],
        compiler_params=_cparams(1),
        name="s5",
    )(u, *consts)


def _t5_bucket_table(n):
    d = np.arange(n)
    max_exact = N_BUCKETS // 2
    nf = np.maximum(d, max_exact).astype(np.float32)
    large = max_exact + (np.log(nf / max_exact) / math.log(MAX_DISTANCE / max_exact)
                         * (N_BUCKETS - max_exact)).astype(np.int32)
    large = np.minimum(large, N_BUCKETS - 1)
    return np.where(d < max_exact, d, large).astype(np.int32)


def _bucket_tiles():
    tbl = _t5_bucket_table(2 * MOBA_BLOCK)
    r = np.arange(MOBA_BLOCK)[:, None]
    c = np.arange(MOBA_BLOCK)[None, :]
    d_own = c - r
    own = np.where(d_own >= 0, tbl[np.maximum(d_own, 0)], -1).astype(np.int32)
    prev = tbl[d_own + MOBA_BLOCK].astype(np.int32)
    return own, prev


def _biastile_kernel(rb_ref, own_ref, prev_ref, o_ref):
    h = pl.program_id(0)
    own = own_ref[...]
    prev = prev_ref[...]
    t_own = jnp.full(own.shape, NEG, F32)
    t_prev = jnp.zeros(prev.shape, F32)
    for b in range(N_BUCKETS):
        v = rb_ref[b, h] * LOG2E
        t_own = jnp.where(own == b, v, t_own)
        t_prev = jnp.where(prev == b, v, t_prev)
    o_ref[0, 0] = t_own
    o_ref[0, 1] = t_prev


def _biastile(rel_bias, own, prev):
    blk = own.shape
    return pl.pallas_call(
        _biastile_kernel,
        grid=(N_HEADS,),
        in_specs=[pl.BlockSpec(memory_space=pltpu.SMEM),
                  pl.BlockSpec(blk, lambda h: (0, 0)), pl.BlockSpec(blk, lambda h: (0, 0))],
        out_specs=pl.BlockSpec((1, 2) + blk, lambda h: (h, 0, 0, 0)),
        out_shape=jax.ShapeDtypeStruct((N_HEADS, 2) + blk, F32),
        compiler_params=_cparams(1),
        name="biastile",
    )(rel_bias, own, prev)


MOBA_BIAS_PARTS = 3
MOBA_ONES_ROWS = 16


def _moba_selector(nb):
    e = np.zeros((nb, MOBA_BLOCK, LANES), np.float32)
    for j in range(nb):
        e[j, :, j] = 1.0
        e[j, :, nb:nb + MOBA_BIAS_PARTS] = 1.0
    return e


def _moba_kernel(rb_ref, qt_ref, k_ref, vt_ref, kmean_ref, bias_ref, esel_ref, o_ref,
                 qpad_ref, sel_ref, m_ref, acc_ref):
    i = pl.program_id(1)
    nb = kmean_ref.shape[1]
    tq = qt_ref.shape[3]
    pair_w = 2 * HEAD_DIM
    heads = range(N_HEADS)

    def pair_cols(h):
        return slice((h // 2) * pair_w, (h // 2 + 1) * pair_w)

    blk = lax.broadcasted_iota(jnp.int32, (nb, tq), 0)
    for h in heads:
        qt = qt_ref[0, 0, h * HEAD_DIM:(h + 1) * HEAD_DIM, :]
        zero = jnp.zeros_like(qt)
        qpad = jnp.concatenate([qt, zero] if h % 2 == 0 else [zero, qt], axis=0)
        gate = jnp.dot(kmean_ref[0, :, pair_cols(h)], qpad, preferred_element_type=F32)
        g = jnp.where(blk < i, gate, -jnp.inf)
        sel = jnp.zeros((nb, tq), F32)
        for _ in range(MOBA_TOPK):
            mx = jnp.max(g, axis=0, keepdims=True)
            cand = jnp.where((g == mx) & (mx > -jnp.inf), blk, nb)
            idx = jnp.min(cand, axis=0, keepdims=True)
            pick = blk == idx
            sel = jnp.where(pick, 1.0, sel)
            g = jnp.where(pick, -jnp.inf, g)
        mask = jnp.where(sel > 0.0, 0.0, NEG)
        sel_ref[h] = mask
        c = jnp.full((nb, tq), rb_ref[N_BUCKETS - 1, h] * LOG2E, F32)
        cb = jnp.zeros((nb, tq), F32)
        for r in range(MOBA_BIAS_PARTS):
            pc = c.astype(BF16).astype(F32)
            cb = jnp.where(blk == r, pc, cb)
            c = c - pc
        pad = jnp.zeros((pair_w - 2 * nb, tq), BF16)
        qpad_ref[h] = jnp.concatenate([qpad, mask.astype(BF16), cb.astype(BF16), pad], axis=0)

    def key_block(h, j):
        r0 = pl.multiple_of(j * MOBA_BLOCK, MOBA_BLOCK)
        return k_ref[0, pl.ds(r0, MOBA_BLOCK), pair_cols(h)]

    def near_scores(h, j, bias):
        return jnp.dot(key_block(h, j), qpad_ref[h, 0:pair_w], preferred_element_type=F32) + bias

    def far_scores(h, j):
        lhs = jnp.concatenate([key_block(h, j), esel_ref[j]], axis=1)
        return jnp.dot(lhs, qpad_ref[h], preferred_element_type=F32)

    ones = jnp.ones((MOBA_ONES_ROWS, MOBA_BLOCK), BF16)

    def tile_pass(tiles, state):
        queue = {}
        for t in range(min(MOBA_LOOKAHEAD, len(tiles))):
            queue[t] = tiles[t][2]()
        for t in range(len(tiles)):
            if t + MOBA_LOOKAHEAD < len(tiles):
                queue[t + MOBA_LOOKAHEAD] = tiles[t + MOBA_LOOKAHEAD][2]()
            h, j, _ = tiles[t]
            s = queue.pop(t)
            m_new = jnp.max(s, axis=0, keepdims=True)
            if h in state:
                m_new = jnp.maximum(state[h][0], m_new)
            p = jnp.exp2(s - m_new).astype(BF16)
            v1 = jnp.concatenate([vt_ref[0, j, h * HEAD_DIM:(h + 1) * HEAD_DIM, :], ones], axis=0)
            acc_new = jnp.dot(v1, p, preferred_element_type=F32)
            if h in state:
                acc_new = jnp.exp2(state[h][0] - m_new) * state[h][1] + acc_new
            state[h] = (m_new, acc_new)
        return state

    def load_state():
        return {h: (m_ref[h], acc_ref[h]) for h in heads}

    def store_state(state):
        for h in heads:
            m_ref[h], acc_ref[h] = state[h]

    def own_tiles():
        return [(h, i, functools.partial(near_scores, h, i, bias_ref[h, 0])) for h in heads]

    @pl.when(i == 0)
    def _():
        store_state(tile_pass(own_tiles(), {}))

    def far_tiles(j0, count):
        return [(h, j0 + jj, functools.partial(far_scores, h, j0 + jj)) for jj in range(count) for h in heads]

    n_far = jnp.maximum(i - 1, 0)
    rem = n_far % MOBA_FAR_GROUP
    for r in range(MOBA_FAR_GROUP):
        @pl.when((i >= 1) & (rem == r))
        def _(r=r):
            j = i - 1
            prev = [(h, j, functools.partial(near_scores, h, j, bias_ref[h, 1] + sel_ref[h, pl.ds(j, 1), :]))
                    for h in heads]
            store_state(tile_pass(own_tiles() + prev + far_tiles(0, r), {}))

    def far_group(it, flat):
        state = tile_pass(far_tiles(rem + MOBA_FAR_GROUP * it, MOBA_FAR_GROUP), {h: flat[h] for h in heads})
        return tuple(state[h] for h in heads)

    state0 = load_state()
    final = lax.fori_loop(0, n_far // MOBA_FAR_GROUP, far_group, tuple(state0[h] for h in heads))

    for h in heads:
        acc = final[h][1]
        o_ref[0, 0, h * HEAD_DIM:(h + 1) * HEAD_DIM, :] = (
            acc[:HEAD_DIM] / acc[HEAD_DIM:HEAD_DIM + 1]).astype(o_ref.dtype)


def _moba(rel_bias, qt, k, vt, kmean, bias_tiles):
    bsz, nb, _, tq = qt.shape
    t = k.shape[1]
    esel = jnp.asarray(_moba_selector(nb), BF16)
    once = lambda a: pl.BlockSpec(a.shape, lambda b, i: (0,) * a.ndim, pipeline_mode=pl.Buffered(1))
    return pl.pallas_call(
        _moba_kernel,
        grid=(bsz, nb),
        in_specs=[
            pl.BlockSpec(memory_space=pltpu.SMEM),
            pl.BlockSpec((1, 1, D_ATTN, tq), lambda b, i: (b, i, 0, 0)),
            pl.BlockSpec((1, t, D_ATTN), lambda b, i: (b, 0, 0)),
            pl.BlockSpec((1, nb, D_ATTN, tq), lambda b, i: (b, 0, 0, 0)),
            pl.BlockSpec((1, nb, D_ATTN), lambda b, i: (b, 0, 0)),
            once(bias_tiles), once(esel),
        ],
        out_specs=pl.BlockSpec((1, 1, D_ATTN, tq), lambda b, i: (b, i, 0, 0)),
        out_shape=jax.ShapeDtypeStruct((bsz, nb, D_ATTN, tq), BF16),
        scratch_shapes=[pltpu.VMEM((N_HEADS, 4 * HEAD_DIM, tq), BF16), pltpu.VMEM((N_HEADS, nb, tq), F32),
                        pltpu.VMEM((N_HEADS, 1, tq), F32),
                        pltpu.VMEM((N_HEADS, HEAD_DIM + MOBA_ONES_ROWS, tq), F32)],
        compiler_params=_cparams(2),
        name="moba",
    )(rel_bias, qt, k, vt, kmean, bias_tiles, esel)


def _memkv_kernel(mem_ref, g_ref, wkv_ref, kg_ref, k_ref, v_ref):
    mn = _rms_rows(mem_ref[0], g_ref[...]).astype(BF16)
    kv = jnp.dot(mn, wkv_ref[...], preferred_element_type=F32)
    ks = []
    for hh in range(X_HEADS):
        cs = slice(hh * X_HEAD_DIM, (hh + 1) * X_HEAD_DIM)
        ks.append(_rms_rows(kv[:, cs], kg_ref[...]))
    k_ref[0] = jnp.concatenate(ks, axis=1).astype(BF16)
    v_ref[0] = kv[:, D_X:].astype(BF16)


def _memkv(mem, g, w_kv, kg):
    bsz, m, d = mem.shape
    full = lambda a: pl.BlockSpec(a.shape, lambda b: (0,) * a.ndim)
    return pl.pallas_call(
        _memkv_kernel,
        grid=(bsz,),
        in_specs=[pl.BlockSpec((1, m, d), lambda b: (b, 0, 0)), full(g), full(w_kv), full(kg)],
        out_specs=[pl.BlockSpec((1, m, D_X), lambda b: (b, 0, 0))] * 2,
        out_shape=[jax.ShapeDtypeStruct((bsz, m, D_X), BF16)] * 2,
        compiler_params=_cparams(1),
        name="memkv",
    )(mem, g, w_kv, kg)


OUTX_PIECE = 512


def _outx_kernel(x_ref, ys_ref, yat_ref, ga_ref, wout_ref, gx_ref, wq_ref, qg_ref, kx_ref, vx_ref,
                 wo_ref, o_ref):
    per_piece = OUTX_PIECE // MOBA_BLOCK
    pieces = range(yat_ref.shape[1] // per_piece)
    rows = [slice(r * OUTX_PIECE, (r + 1) * OUTX_PIECE) for r in pieces]
    ya_n = []
    for r in pieces:
        parts = []
        for blk in range(r * per_piece, (r + 1) * per_piece):
            ya = yat_ref[0, blk].astype(F32)
            ms = jnp.mean(ya * ya, axis=0, keepdims=True)
            yan = ya * lax.rsqrt(ms + EPS) * jnp.tile(ga_ref[...], (1, ya.shape[1] // LANES))
            parts.append(yan.T)
        ya_n.append(jnp.concatenate(parts, axis=0).astype(BF16))
    h1 = [x_ref[0, rows[r]]
          + jnp.dot(ys_ref[0, rows[r]], wout_ref[:D_SSM], preferred_element_type=F32)
          + jnp.dot(ya_n[r], wout_ref[D_SSM:], preferred_element_type=F32) for r in pieces]
    hn = [_rms_rows(h1[r], gx_ref[...]).astype(BF16) for r in pieces]
    q = [jnp.dot(hn[r], wq_ref[...], preferred_element_type=F32) for r in pieces]
    outs = [[] for _ in pieces]
    head_cols = [slice(hh * X_HEAD_DIM, (hh + 1) * X_HEAD_DIM) for hh in range(X_HEADS)]

    def scores(hh):
        cs = head_cols[hh]
        qh = [_rms_rows(q[r][:, cs], qg_ref[...]).astype(BF16) for r in pieces]
        return [lax.dot_general(qh[r], kx_ref[0, :, cs], (((1,), (1,)), ((), ())),
                                preferred_element_type=F32) * (X_HEAD_DIM ** -0.5) for r in pieces]

    s_next = scores(0)
    for hh in range(X_HEADS):
        s = s_next
        if hh + 1 < X_HEADS:
            s_next = scores(hh + 1)
        for r in pieces:
            e = jnp.exp(s[r] - jnp.max(s[r], axis=-1, keepdims=True))
            prob = e / jnp.sum(e, axis=-1, keepdims=True)
            outs[r].append(jnp.dot(prob.astype(BF16), vx_ref[0, :, head_cols[hh]],
                                   preferred_element_type=F32))
    for r in pieces:
        o = jnp.concatenate(outs[r], axis=1).astype(BF16)
        o_ref[0, rows[r]] = h1[r] + jnp.dot(o, wo_ref[...], preferred_element_type=F32)


def _outx(x, ys, yat, ga, w_out, gx, w_q, qg, kx, vx, w_o, tm):
    bsz, t, d = x.shape
    m = kx.shape[1]
    rb = tm // MOBA_BLOCK
    full = lambda a: pl.BlockSpec(a.shape, lambda b, i: (0,) * a.ndim)
    return pl.pallas_call(
        _outx_kernel,
        grid=(bsz, t // tm),
        in_specs=[
            pl.BlockSpec((1, tm, d), lambda b, i: (b, i, 0)),
            pl.BlockSpec((1, tm, D_SSM), lambda b, i: (b, i, 0)),
            pl.BlockSpec((1, rb, D_ATTN, MOBA_BLOCK), lambda b, i: (b, i, 0, 0)),
            full(ga), full(w_out), full(gx), full(w_q), full(qg),
            pl.BlockSpec((1, m, D_X), lambda b, i: (b, 0, 0)),
            pl.BlockSpec((1, m, D_X), lambda b, i: (b, 0, 0)),
            full(w_o),
        ],
        out_specs=pl.BlockSpec((1, tm, d), lambda b, i: (b, i, 0)),
        out_shape=jax.ShapeDtypeStruct((bsz, t, d), F32),
        compiler_params=_cparams(2),
        name="outx",
    )(x, ys, yat, ga, w_out, gx, w_q, qg, kx, vx, w_o)


FFN_CHUNK = 1408


def _ffn_kernel(h_ref, g_ref, wup_ref, cw_ref, cb_ref, wdn_ref, o_ref, tail_ref):
    tm = h_ref.shape[1]

    @pl.when(pl.program_id(1) == 0)
    def _():
        tail_ref[...] = jnp.zeros_like(tail_ref)

    h = h_ref[0]
    hn = _rms_rows(h, g_ref[...]).astype(BF16)
    acc = h

    def conv(up, cols):
        ext = jnp.concatenate([tail_ref[:, cols], up], axis=0)
        tail_ref[:, cols] = up[tm - SUBLANES:]
        w = cw_ref[:, cols]
        return (w[0:1] * ext[SUBLANES - 2:SUBLANES - 2 + tm]
                + w[1:2] * ext[SUBLANES - 1:SUBLANES - 1 + tm]
                + w[2:3] * up + cb_ref[:, cols])

    n_chunks = D_FF // FFN_CHUNK
    cols = [(slice(c * FFN_CHUNK, (c + 1) * FFN_CHUNK),
             slice(D_FF + c * FFN_CHUNK, D_FF + (c + 1) * FFN_CHUNK)) for c in range(n_chunks)]
    ups = [tuple(jnp.dot(hn, wup_ref[:, cs], preferred_element_type=F32) for cs in cols[c])
           for c in range(n_chunks)]
    for c in range(n_chunks):
        act = (jax.nn.silu(conv(ups[c][0], cols[c][0])) * conv(ups[c][1], cols[c][1])).astype(BF16)
        acc = acc + jnp.dot(act, wdn_ref[cols[c][0]], preferred_element_type=F32)
    o_ref[0] = acc


def _ffn(h, g, w_up, conv_w, conv_b, w_down, tm):
    bsz, t, d = h.shape
    full = lambda a: pl.BlockSpec(a.shape, lambda b, i: (0,) * a.ndim, pipeline_mode=pl.Buffered(1))
    return pl.pallas_call(
        _ffn_kernel,
        grid=(bsz, t // tm),
        in_specs=[pl.BlockSpec((1, tm, d), lambda b, i: (b, i, 0)),
                  full(g), full(w_up), full(conv_w), full(conv_b), full(w_down)],
        out_specs=pl.BlockSpec((1, tm, d), lambda b, i: (b, i, 0)),
        out_shape=jax.ShapeDtypeStruct((bsz, t, d), F32),
        scratch_shapes=[pltpu.VMEM((SUBLANES, 2 * D_FF), F32)],
        compiler_params=_cparams(2),
        name="ffn",
    )(h, g, w_up, conv_w, conv_b, w_down)


def _block_diag_b(bb):
    g, c, n = bb.shape
    per_tile = S5_NT // n
    per_slab = LANES // c
    tiles = S5_COLS // S5_NT
    place = np.zeros((tiles, per_slab, per_tile), np.float32)
    for j in range(tiles):
        for q in range(per_tile):
            place[j, (j * per_tile + q) % per_slab, q] = 1.0
    out = jnp.einsum('jqcn,jgq->jgcqn', bb.reshape(tiles, per_tile, c, n), jnp.asarray(place))
    return out.reshape(tiles, LANES, S5_NT).astype(BF16)


def _block_diag_c(cm):
    g, c, n = cm.shape
    per_tile = g // 2
    out = jnp.einsum('tqcn,qp->tqnpc', cm.reshape(2, per_tile, c, n), jnp.eye(per_tile, dtype=F32))
    return out.reshape(2, per_tile * n, per_tile * c).astype(BF16)


def kernel(x, mem, norm_mix, w_in, attn_qn, attn_kn, ssm_lam_re, ssm_lam_im, ssm_log_step, ssm_b_re, ssm_b_im, ssm_c_re, ssm_c_im, ssm_d, ssm_w_glu, ssm_b_glu, gnorm_ssm, gnorm_attn, w_out, rel_bias, norm_xattn, norm_mem, x_wq, x_wkv, x_wo, x_qn, x_kn, norm_ffn, ffn_w_up, ffn_conv_w, ffn_conv_b, ffn_w_down):
    bsz, t, d = x.shape
    assert bsz == SUBLANES and t % (2 * max(INPROJ_PIECE, OUTX_PIECE)) == 0
    assert norm_mix.shape[0] == 1
    l = 0
    row = lambda v: v.reshape(1, -1).astype(F32)

    w = w_in[l]
    w_u = w[:, :D_SSM].astype(BF16)
    w_kqvt = jnp.concatenate([w[:, D_SSM + D_ATTN:D_SSM + 2 * D_ATTN], w[:, D_SSM:D_SSM + D_ATTN],
                              w[:, D_SSM + 2 * D_ATTN:]], axis=1).T.astype(BF16)
    lane_rep = lambda v: jnp.broadcast_to(v[:, None], (v.shape[0], LANES)).astype(F32)
    qg = lane_rep(jnp.tile(attn_qn[l], N_HEADS) * (HEAD_DIM ** -0.5))
    kg = lane_rep(jnp.tile(attn_kn[l], N_HEADS))
    u, k, kmean, qt, vt = _inproj(x, row(norm_mix[l]), w_u, w_kqvt, kg, qg, tm=2 * INPROJ_PIECE)

    a_re, a_im, bb_re, bb_im = _s5prep(ssm_lam_re[l], ssm_lam_im[l], ssm_log_step[l][:, None],
                                       jnp.swapaxes(ssm_b_re[l], 1, 2), jnp.swapaxes(ssm_b_im[l], 1, 2))
    a_re8 = jnp.broadcast_to(a_re.reshape(1, -1), (SUBLANES, S5_COLS))
    a_im8 = jnp.broadcast_to(a_im.reshape(1, -1), (SUBLANES, S5_COLS))
    ys = _s5(u, _block_diag_b(bb_re), _block_diag_b(bb_im), a_re8, a_im8,
             _block_diag_c(ssm_c_re[l]), _block_diag_c(ssm_c_im[l]), row(ssm_d[l]),
             ssm_w_glu[l].astype(BF16), row(ssm_b_glu[l]), row(gnorm_ssm[l]), tt=S5_TIME_TILE)

    own, prev = _bucket_tiles()
    bias_tiles = _biastile(rel_bias, jnp.asarray(own), jnp.asarray(prev))
    nb = t // MOBA_BLOCK
    yat = _moba(rel_bias, qt, k, vt, kmean.reshape(bsz, nb, D_ATTN).astype(BF16), bias_tiles)

    kx, vx = _memkv(mem, row(norm_mem[l]), x_wkv[l].astype(BF16), row(x_kn[l]))
    ga = jnp.broadcast_to(gnorm_attn[l][:, None], (D_ATTN, LANES)).astype(F32)
    h2 = _outx(x, ys, yat, ga, w_out[l].astype(BF16), row(norm_xattn[l]), x_wq[l].astype(BF16),
               row(x_qn[l]), kx, vx, x_wo[l].astype(BF16), tm=2 * OUTX_PIECE)

    return _ffn(h2, row(norm_ffn[l]), ffn_w_up[l].astype(BF16), ffn_conv_w[l].reshape(3, 2 * D_FF),
                row(ffn_conv_b[l]), ffn_w_down[l].astype(BF16), tm=FFN_ROWS)
```

```python
import functools
import math

import numpy as np
import jax
import jax.numpy as jnp
from jax import lax
from jax.experimental import pallas as pl
from jax.experimental.pallas import tpu as pltpu

F32 = jnp.float32
BF16 = jnp.bfloat16

EPS = 1e-6
NEG = -1e30
LOG2E = 1.4426950408889634

D_SSM = 512
N_GROUPS = 32
SSM_STATE = 64
D_ATTN = 512
HEAD_DIM = 64
N_HEADS = 8
MOBA_BLOCK = 256
MOBA_TOPK = 3
MOBA_LOOKAHEAD = 5
MOBA_FAR_GROUP = 8
N_BUCKETS = 32
MAX_DISTANCE = 128
X_HEADS = 4
X_HEAD_DIM = 128
D_X = 512
D_FF = 2816
LANES = 128
SUBLANES = 8
V7X_VMEM_BYTES = 64 * 1024 * 1024
VMEM_LIMIT = V7X_VMEM_BYTES * 7 // 8

S5_TIME_TILE = 128
FFN_ROWS = 512


def _cparams(n_axes):
    return pltpu.CompilerParams(
        dimension_semantics=("arbitrary",) * n_axes, vmem_limit_bytes=VMEM_LIMIT)


def _rms_rows(x, g):
    ms = jnp.mean(x * x, axis=-1, keepdims=True)
    return x * lax.rsqrt(ms + EPS) * g


INPROJ_PIECE = 512


def _inproj_kernel(x_ref, g_ref, wu_ref, wkqvt_ref, kg_ref, qg_ref,
                   u_ref, k_ref, kmean_ref, qt_ref, vt_ref):
    tp = INPROJ_PIECE
    pieces = range(x_ref.shape[1] // tp)
    per_piece = tp // MOBA_BLOCK
    rows = [slice(r * tp, (r + 1) * tp) for r in pieces]
    hn = [_rms_rows(x_ref[0, rows[r]], g_ref[...]).astype(BF16) for r in pieces]
    p_u, p_t = [], []
    for r in pieces:
        p_u.append(jnp.dot(hn[r], wu_ref[...], preferred_element_type=F32))
        p_t.append(lax.dot_general(wkqvt_ref[...], hn[r], (((1,), (1,)), ((), ())),
                                   preferred_element_type=F32))

    def head_norm_t(xt, gain_ref):
        x3 = xt.reshape(N_HEADS, HEAD_DIM, tp)
        ms = jnp.mean(x3 * x3, axis=1, keepdims=True)
        return (x3 * lax.rsqrt(ms + EPS)).reshape(D_ATTN, tp) * jnp.tile(gain_ref[...], (1, tp // LANES))

    for r in pieces:
        u_ref[0, rows[r]] = p_u[r]
        kn = head_norm_t(p_t[r][:D_ATTN], kg_ref).T
        k_ref[0, rows[r]] = (kn * LOG2E).astype(BF16)
        qn = head_norm_t(p_t[r][D_ATTN:2 * D_ATTN], qg_ref)
        vt = p_t[r][2 * D_ATTN:]
        for b in range(per_piece):
            cs = slice(b * MOBA_BLOCK, (b + 1) * MOBA_BLOCK)
            qt_ref[0, r * per_piece + b] = qn[:, cs].astype(BF16)
            vt_ref[0, r * per_piece + b] = vt[:, cs].astype(BF16)
            kmean_ref[0, r * per_piece + b] = jnp.mean(kn[cs], axis=0, keepdims=True)


def _inproj(x, g, w_u, w_kqvt, kg, qg, tm):
    bsz, t, d = x.shape
    nb = t // MOBA_BLOCK
    rb = tm // MOBA_BLOCK
    full = lambda shp: pl.BlockSpec(shp, lambda b, i: (0,) * len(shp))
    return pl.pallas_call(
        _inproj_kernel,
        grid=(bsz, t // tm),
        in_specs=[
            pl.BlockSpec((1, tm, d), lambda b, i: (b, i, 0)),
            full((1, d)), full(w_u.shape), full(w_kqvt.shape), full(kg.shape), full(qg.shape),
        ],
        out_specs=[
            pl.BlockSpec((1, tm, D_SSM), lambda b, i: (b, i, 0)),
            pl.BlockSpec((1, tm, D_ATTN), lambda b, i: (b, i, 0)),
            pl.BlockSpec((1, rb, 1, D_ATTN), lambda b, i: (b, i, 0, 0)),
            pl.BlockSpec((1, rb, D_ATTN, MOBA_BLOCK), lambda b, i: (b, i, 0, 0)),
            pl.BlockSpec((1, rb, D_ATTN, MOBA_BLOCK), lambda b, i: (b, i, 0, 0)),
        ],
        out_shape=[
            jax.ShapeDtypeStruct((bsz, t, D_SSM), F32),
            jax.ShapeDtypeStruct((bsz, t, D_ATTN), BF16),
            jax.ShapeDtypeStruct((bsz, nb, 1, D_ATTN), F32),
            jax.ShapeDtypeStruct((bsz, nb, D_ATTN, MOBA_BLOCK), BF16),
            jax.ShapeDtypeStruct((bsz, nb, D_ATTN, MOBA_BLOCK), BF16),
        ],
        compiler_params=_cparams(2),
        name="inproj",
    )(x, g, w_u, w_kqvt, kg, qg)


def _s5prep_kernel(lre_ref, lim_ref, ls_ref, bre_ref, bim_ref, are_ref, aim_ref, bbre_ref, bbim_ref):
    a_re = jnp.minimum(lre_ref[...], -1e-4)
    a_im = lim_ref[...]
    step = jnp.exp(ls_ref[...])
    mag = jnp.exp(step * a_re)
    ab_re = mag * jnp.cos(step * a_im)
    ab_im = mag * jnp.sin(step * a_im)
    den = a_re * a_re + a_im * a_im
    p = ab_re - 1.0
    f_re = (p * a_re + ab_im * a_im) / den
    f_im = (ab_im * a_re - p * a_im) / den
    are_ref[...] = ab_re
    aim_ref[...] = ab_im
    bbre_ref[...] = f_re[:, None, :] * bre_ref[...] - f_im[:, None, :] * bim_ref[...]
    bbim_ref[...] = f_re[:, None, :] * bim_ref[...] + f_im[:, None, :] * bre_ref[...]


def _s5prep(lam_re, lam_im, log_step, b_re_t, b_im_t):
    g, n = lam_re.shape
    c = b_re_t.shape[1]
    return pl.pallas_call(
        _s5prep_kernel,
        out_shape=[jax.ShapeDtypeStruct((g, n), F32), jax.ShapeDtypeStruct((g, n), F32),
                   jax.ShapeDtypeStruct((g, c, n), F32), jax.ShapeDtypeStruct((g, c, n), F32)],
        name="s5prep",
    )(lam_re, lam_im, log_step, b_re_t, b_im_t)


S5_COLS = N_GROUPS * SSM_STATE
S5_NT = 256
S5_SCAN_COLS = 512
S5_PARTS = 4


def _s5_kernel(u_ref, bwre_ref, bwim_ref, are_ref, aim_ref, cwre_ref, cwim_ref, d_ref, wglu_ref,
               bglu_ref, gn_ref, o_ref, sre_ref, sim_ref, stre_ref, stim_ref, ut_ref, yt_ref):
    bsz, tt, _ = u_ref.shape
    n_slab = D_SSM // LANES

    @pl.when(pl.program_id(0) == 0)
    def _():
        stre_ref[...] = jnp.zeros_like(stre_ref)
        stim_ref[...] = jnp.zeros_like(stim_ref)

    for b in range(bsz):
        for s in range(n_slab):
            ut_ref[s, pl.ds(b, tt, stride=bsz), :] = u_ref[b, :, s * LANES:(s + 1) * LANES]
    n_parts = S5_PARTS
    tq = tt // n_parts
    n_chunks = S5_COLS // S5_SCAN_COLS
    chunks = [slice(cc * S5_SCAN_COLS, (cc + 1) * S5_SCAN_COLS) for cc in range(n_chunks)]
    a_re = [are_ref[:, cs] for cs in chunks]
    a_im = [aim_ref[:, cs] for cs in chunks]
    state = [(stre_ref[:, cs], stim_ref[:, cs]) for cs in chunks]
    half = S5_COLS // 2

    def part_rows(q):
        return slice(q * tq * bsz, (q + 1) * tq * bsz)

    def bu(q):
        rs = part_rows(q)
        slabs = [ut_ref[s, rs, :].astype(BF16) for s in range(n_slab)]
        for j in range(S5_COLS // S5_NT):
            cs = slice(j * S5_NT, (j + 1) * S5_NT)
            sre_ref[rs, cs] = jnp.dot(slabs[j // 2], bwre_ref[j], preferred_element_type=F32)
            sim_ref[rs, cs] = jnp.dot(slabs[j // 2], bwim_ref[j], preferred_element_type=F32)

    def scan(q):
        for cc, cs in enumerate(chunks):
            s_re, s_im = state[cc]
            for t in range(q * tq, (q + 1) * tq):
                rs = slice(t * bsz, (t + 1) * bsz)
                n_re = a_re[cc] * s_re - a_im[cc] * s_im + sre_ref[rs, cs]
                n_im = a_re[cc] * s_im + a_im[cc] * s_re + sim_ref[rs, cs]
                sre_ref[rs, cs] = n_re
                sim_ref[rs, cs] = n_im
                s_re, s_im = n_re, n_im
            state[cc] = (s_re, s_im)

    y_parts = {}

    def cs_out(q):
        rs = part_rows(q)
        ys = []
        for n in range(2):
            ks = slice(n * half, (n + 1) * half)
            ys.append(jnp.dot(sre_ref[rs, ks].astype(BF16), cwre_ref[n], preferred_element_type=F32)
                      - jnp.dot(sim_ref[rs, ks].astype(BF16), cwim_ref[n], preferred_element_type=F32))
        u = jnp.concatenate([ut_ref[s, rs, :] for s in range(n_slab)], axis=1)
        y_parts[q] = jax.nn.gelu(jnp.concatenate(ys, axis=1) + d_ref[...] * u)

    def glu(q):
        rs = part_rows(q)
        y = y_parts.pop(q)
        z = jnp.dot(y.astype(BF16), wglu_ref[...], preferred_element_type=F32) + bglu_ref[...]
        yn = _rms_rows(y * jax.nn.sigmoid(z), gn_ref[...])
        for s in range(n_slab):
            yt_ref[s, rs, :] = yn[:, s * LANES:(s + 1) * LANES]

    bu(0)
    for q in range(n_parts + 3):
        if q + 1 < n_parts:
            bu(q + 1)
        if q < n_parts:
            scan(q)
        if 0 <= q - 1 < n_parts:
            cs_out(q - 1)
        if 0 <= q - 2 < n_parts:
            glu(q - 2)
    for cc, cs in enumerate(chunks):
        stre_ref[:, cs], stim_ref[:, cs] = state[cc]

    for b in range(bsz):
        for s in range(n_slab):
            o_ref[b, :, s * LANES:(s + 1) * LANES] = yt_ref[s, pl.ds(b, tt, stride=bsz), :].astype(o_ref.dtype)


def _s5(u, bw_re, bw_im, a_re8, a_im8, cw_re, cw_im, d, w_glu, b_glu, gn, tt):
    bsz, t, _ = u.shape
    assert bsz == SUBLANES
    rows = tt * bsz
    full = lambda a: pl.BlockSpec(a.shape, lambda i: (0,) * a.ndim)
    consts = (bw_re, bw_im, a_re8, a_im8, cw_re, cw_im, d, w_glu, b_glu, gn)
    slab_rows = pltpu.VMEM((D_SSM // LANES, rows, LANES), F32)
    return pl.pallas_call(
        _s5_kernel,
        grid=(t // tt,),
        in_specs=[pl.BlockSpec((bsz, tt, D_SSM), lambda i: (0, i, 0))] + [full(a) for a in consts],
        out_specs=pl.BlockSpec((bsz, tt, D_SSM), lambda i: (0, i, 0)),
        out_shape=jax.ShapeDtypeStruct((bsz, t, D_SSM), BF16),
        scratch_shapes=[pltpu.VMEM((rows, S5_COLS), F32), pltpu.VMEM((rows, S5_COLS), F32),
                        pltpu.VMEM((bsz, S5_COLS), F32), pltpu.VMEM((bsz, S5_COLS), F32),
                        slab_rows, slab_rows],
        compiler_params=_cparams(1),
        name="s5",
    )(u, *consts)


def _t5_bucket_table(n):
    d = np.arange(n)
    max_exact = N_BUCKETS // 2
    nf = np.maximum(d, max_exact).astype(np.float32)
    large = max_exact + (np.log(nf / max_exact) / math.log(MAX_DISTANCE / max_exact)
                         * (N_BUCKETS - max_exact)).astype(np.int32)
    large = np.minimum(large, N_BUCKETS - 1)
    return np.where(d < max_exact, d, large).astype(np.int32)


def _bucket_tiles():
    tbl = _t5_bucket_table(2 * MOBA_BLOCK)
    r = np.arange(MOBA_BLOCK)[:, None]
    c = np.arange(MOBA_BLOCK)[None, :]
    d_own = c - r
    own = np.where(d_own >= 0, tbl[np.maximum(d_own, 0)], -1).astype(np.int32)
    prev = tbl[d_own + MOBA_BLOCK].astype(np.int32)
    return own, prev


def _biastile_kernel(rb_ref, own_ref, prev_ref, o_ref):
    h = pl.program_id(0)
    own = own_ref[...]
    prev = prev_ref[...]
    t_own = jnp.full(own.shape, NEG, F32)
    t_prev = jnp.zeros(prev.shape, F32)
    for b in range(N_BUCKETS):
        v = rb_ref[b, h] * LOG2E
        t_own = jnp.where(own == b, v, t_own)
        t_prev = jnp.where(prev == b, v, t_prev)
    o_ref[0, 0] = t_own
    o_ref[0, 1] = t_prev


def _biastile(rel_bias, own, prev):
    blk = own.shape
    return pl.pallas_call(
        _biastile_kernel,
        grid=(N_HEADS,),
        in_specs=[pl.BlockSpec(memory_space=pltpu.SMEM),
                  pl.BlockSpec(blk, lambda h: (0, 0)), pl.BlockSpec(blk, lambda h: (0, 0))],
        out_specs=pl.BlockSpec((1, 2) + blk, lambda h: (h, 0, 0, 0)),
        out_shape=jax.ShapeDtypeStruct((N_HEADS, 2) + blk, F32),
        compiler_params=_cparams(1),
        name="biastile",
    )(rel_bias, own, prev)


MOBA_BIAS_PARTS = 3
MOBA_ONES_ROWS = 16


def _moba_selector(nb):
    e = np.zeros((nb, MOBA_BLOCK, LANES), np.float32)
    for j in range(nb):
        e[j, :, j] = 1.0
        e[j, :, nb:nb + MOBA_BIAS_PARTS] = 1.0
    return e


def _moba_kernel(rb_ref, qt_ref, k_ref, vt_ref, kmean_ref, bias_ref, esel_ref, o_ref,
                 qpad_ref, sel_ref, m_ref, acc_ref):
    i = pl.program_id(1)
    nb = kmean_ref.shape[1]
    tq = qt_ref.shape[3]
    pair_w = 2 * HEAD_DIM
    heads = range(N_HEADS)

    def pair_cols(h):
        return slice((h // 2) * pair_w, (h // 2 + 1) * pair_w)

    blk = lax.broadcasted_iota(jnp.int32, (nb, tq), 0)
    for h in heads:
        qt = qt_ref[0, 0, h * HEAD_DIM:(h + 1) * HEAD_DIM, :]
        zero = jnp.zeros_like(qt)
        qpad = jnp.concatenate([qt, zero] if h % 2 == 0 else [zero, qt], axis=0)
        gate = jnp.dot(kmean_ref[0, :, pair_cols(h)], qpad, preferred_element_type=F32)
        g = jnp.where(blk < i, gate, -jnp.inf)
        sel = jnp.zeros((nb, tq), F32)
        for _ in range(MOBA_TOPK):
            mx = jnp.max(g, axis=0, keepdims=True)
            cand = jnp.where((g == mx) & (mx > -jnp.inf), blk, nb)
            idx = jnp.min(cand, axis=0, keepdims=True)
            pick = blk == idx
            sel = jnp.where(pick, 1.0, sel)
            g = jnp.where(pick, -jnp.inf, g)
        mask = jnp.where(sel > 0.0, 0.0, NEG)
        sel_ref[h] = mask
        c = jnp.full((nb, tq), rb_ref[N_BUCKETS - 1, h] * LOG2E, F32)
        cb = jnp.zeros((nb, tq), F32)
        for r in range(MOBA_BIAS_PARTS):
            pc = c.astype(BF16).astype(F32)
            cb = jnp.where(blk == r, pc, cb)
            c = c - pc
        pad = jnp.zeros((pair_w - 2 * nb, tq), BF16)
        qpad_ref[h] = jnp.concatenate([qpad, mask.astype(BF16), cb.astype(BF16), pad], axis=0)

    def key_block(h, j):
        r0 = pl.multiple_of(j * MOBA_BLOCK, MOBA_BLOCK)
        return k_ref[0, pl.ds(r0, MOBA_BLOCK), pair_cols(h)]

    def near_scores(h, j, bias):
        return jnp.dot(key_block(h, j), qpad_ref[h, 0:pair_w], preferred_element_type=F32) + bias

    def far_scores(h, j):
        lhs = jnp.concatenate([key_block(h, j), esel_ref[j]], axis=1)
        return jnp.dot(lhs, qpad_ref[h], preferred_element_type=F32)

    ones = jnp.ones((MOBA_ONES_ROWS, MOBA_BLOCK), BF16)

    def tile_pass(tiles, state):
        queue = {}
        for t in range(min(MOBA_LOOKAHEAD, len(tiles))):
            queue[t] = tiles[t][2]()
        for t in range(len(tiles)):
            if t + MOBA_LOOKAHEAD < len(tiles):
                queue[t + MOBA_LOOKAHEAD] = tiles[t + MOBA_LOOKAHEAD][2]()
            h, j, _ = tiles[t]
            s = queue.pop(t)
            m_new = jnp.max(s, axis=0, keepdims=True)
            if h in state:
                m_new = jnp.maximum(state[h][0], m_new)
            p = jnp.exp2(s - m_new).astype(BF16)
            v1 = jnp.concatenate([vt_ref[0, j, h * HEAD_DIM:(h + 1) * HEAD_DIM, :], ones], axis=0)
            acc_new = jnp.dot(v1, p, preferred_element_type=F32)
            if h in state:
                acc_new = jnp.exp2(state[h][0] - m_new) * state[h][1] + acc_new
            state[h] = (m_new, acc_new)
        return state

    def load_state():
        return {h: (m_ref[h], acc_ref[h]) for h in heads}

    def store_state(state):
        for h in heads:
            m_ref[h], acc_ref[h] = state[h]

    def own_tiles():
        return [(h, i, functools.partial(near_scores, h, i, bias_ref[h, 0])) for h in heads]

    @pl.when(i == 0)
    def _():
        store_state(tile_pass(own_tiles(), {}))

    def far_tiles(j0, count):
        return [(h, j0 + jj, functools.partial(far_scores, h, j0 + jj)) for jj in range(count) for h in heads]

    n_far = jnp.maximum(i - 1, 0)
    rem = n_far % MOBA_FAR_GROUP
    for r in range(MOBA_FAR_GROUP):
        @pl.when((i >= 1) & (rem == r))
        def _(r=r):
            j = i - 1
            prev = [(h, j, functools.partial(near_scores, h, j, bias_ref[h, 1] + sel_ref[h, pl.ds(j, 1), :]))
                    for h in heads]
            store_state(tile_pass(own_tiles() + prev + far_tiles(0, r), {}))

    def far_group(it, flat):
        state = tile_pass(far_tiles(rem + MOBA_FAR_GROUP * it, MOBA_FAR_GROUP), {h: flat[h] for h in heads})
        return tuple(state[h] for h in heads)

    state0 = load_state()
    final = lax.fori_loop(0, n_far // MOBA_FAR_GROUP, far_group, tuple(state0[h] for h in heads))

    for h in heads:
        acc = final[h][1]
        o_ref[0, 0, h * HEAD_DIM:(h + 1) * HEAD_DIM, :] = (
            acc[:HEAD_DIM] / acc[HEAD_DIM:HEAD_DIM + 1]).astype(o_ref.dtype)


def _moba(rel_bias, qt, k, vt, kmean, bias_tiles):
    bsz, nb, _, tq = qt.shape
    t = k.shape[1]
    esel = jnp.asarray(_moba_selector(nb), BF16)
    once = lambda a: pl.BlockSpec(a.shape, lambda b, i: (0,) * a.ndim, pipeline_mode=pl.Buffered(1))
    return pl.pallas_call(
        _moba_kernel,
        grid=(bsz, nb),
        in_specs=[
            pl.BlockSpec(memory_space=pltpu.SMEM),
            pl.BlockSpec((1, 1, D_ATTN, tq), lambda b, i: (b, i, 0, 0)),
            pl.BlockSpec((1, t, D_ATTN), lambda b, i: (b, 0, 0)),
            pl.BlockSpec((1, nb, D_ATTN, tq), lambda b, i: (b, 0, 0, 0)),
            pl.BlockSpec((1, nb, D_ATTN), lambda b, i: (b, 0, 0)),
            once(bias_tiles), once(esel),
        ],
        out_specs=pl.BlockSpec((1, 1, D_ATTN, tq), lambda b, i: (b, i, 0, 0)),
        out_shape=jax.ShapeDtypeStruct((bsz, nb, D_ATTN, tq), BF16),
        scratch_shapes=[pltpu.VMEM((N_HEADS, 4 * HEAD_DIM, tq), BF16), pltpu.VMEM((N_HEADS, nb, tq), F32),
                        pltpu.VMEM((N_HEADS, 1, tq), F32),
                        pltpu.VMEM((N_HEADS, HEAD_DIM + MOBA_ONES_ROWS, tq), F32)],
        compiler_params=_cparams(2),
        name="moba",
    )(rel_bias, qt, k, vt, kmean, bias_tiles, esel)


def _memkv_kernel(mem_ref, g_ref, wkv_ref, kg_ref, k_ref, v_ref):
    mn = _rms_rows(mem_ref[0], g_ref[...]).astype(BF16)
    kv = jnp.dot(mn, wkv_ref[...], preferred_element_type=F32)
    ks = []
    for hh in range(X_HEADS):
        cs = slice(hh * X_HEAD_DIM, (hh + 1) * X_HEAD_DIM)
        ks.append(_rms_rows(kv[:, cs], kg_ref[...]))
    k_ref[0] = jnp.concatenate(ks, axis=1).astype(BF16)
    v_ref[0] = kv[:, D_X:].astype(BF16)


def _memkv(mem, g, w_kv, kg):
    bsz, m, d = mem.shape
    full = lambda a: pl.BlockSpec(a.shape, lambda b: (0,) * a.ndim)
    return pl.pallas_call(
        _memkv_kernel,
        grid=(bsz,),
        in_specs=[pl.BlockSpec((1, m, d), lambda b: (b, 0, 0)), full(g), full(w_kv), full(kg)],
        out_specs=[pl.BlockSpec((1, m, D_X), lambda b: (b, 0, 0))] * 2,
        out_shape=[jax.ShapeDtypeStruct((bsz, m, D_X), BF16)] * 2,
        compiler_params=_cparams(1),
        name="memkv",
    )(mem, g, w_kv, kg)


OUTX_PIECE = 512


def _outx_kernel(x_ref, ys_ref, yat_ref, ga_ref, wout_ref, gx_ref, wq_ref, qg_ref, kx_ref, vx_ref,
                 wo_ref, o_ref):
    per_piece = OUTX_PIECE // MOBA_BLOCK
    pieces = range(yat_ref.shape[1] // per_piece)
    rows = [slice(r * OUTX_PIECE, (r + 1) * OUTX_PIECE) for r in pieces]
    ya_n = []
    for r in pieces:
        parts = []
        for blk in range(r * per_piece, (r + 1) * per_piece):
            ya = yat_ref[0, blk].astype(F32)
            ms = jnp.mean(ya * ya, axis=0, keepdims=True)
            yan = ya * lax.rsqrt(ms + EPS) * jnp.tile(ga_ref[...], (1, ya.shape[1] // LANES))
            parts.append(yan.T)
        ya_n.append(jnp.concatenate(parts, axis=0).astype(BF16))
    h1 = [x_ref[0, rows[r]]
          + jnp.dot(ys_ref[0, rows[r]], wout_ref[:D_SSM], preferred_element_type=F32)
          + jnp.dot(ya_n[r], wout_ref[D_SSM:], preferred_element_type=F32) for r in pieces]
    hn = [_rms_rows(h1[r], gx_ref[...]).astype(BF16) for r in pieces]
    q = [jnp.dot(hn[r], wq_ref[...], preferred_element_type=F32) for r in pieces]
    outs = [[] for _ in pieces]
    head_cols = [slice(hh * X_HEAD_DIM, (hh + 1) * X_HEAD_DIM) for hh in range(X_HEADS)]

    def scores(hh):
        cs = head_cols[hh]
        qh = [_rms_rows(q[r][:, cs], qg_ref[...]).astype(BF16) for r in pieces]
        return [lax.dot_general(qh[r], kx_ref[0, :, cs], (((1,), (1,)), ((), ())),
                                preferred_element_type=F32) * (X_HEAD_DIM ** -0.5) for r in pieces]

    s_next = scores(0)
    for hh in range(X_HEADS):
        s = s_next
        if hh + 1 < X_HEADS:
            s_next = scores(hh + 1)
        for r in pieces:
            e = jnp.exp(s[r] - jnp.max(s[r], axis=-1, keepdims=True))
            prob = e / jnp.sum(e, axis=-1, keepdims=True)
            outs[r].append(jnp.dot(prob.astype(BF16), vx_ref[0, :, head_cols[hh]],
                                   preferred_element_type=F32))
    for r in pieces:
        o = jnp.concatenate(outs[r], axis=1).astype(BF16)
        o_ref[0, rows[r]] = h1[r] + jnp.dot(o, wo_ref[...], preferred_element_type=F32)


def _outx(x, ys, yat, ga, w_out, gx, w_q, qg, kx, vx, w_o, tm):
    bsz, t, d = x.shape
    m = kx.shape[1]
    rb = tm // MOBA_BLOCK
    full = lambda a: pl.BlockSpec(a.shape, lambda b, i: (0,) * a.ndim)
    return pl.pallas_call(
        _outx_kernel,
        grid=(bsz, t // tm),
        in_specs=[
            pl.BlockSpec((1, tm, d), lambda b, i: (b, i, 0)),
            pl.BlockSpec((1, tm, D_SSM), lambda b, i: (b, i, 0)),
            pl.BlockSpec((1, rb, D_ATTN, MOBA_BLOCK), lambda b, i: (b, i, 0, 0)),
            full(ga), full(w_out), full(gx), full(w_q), full(qg),
            pl.BlockSpec((1, m, D_X), lambda b, i: (b, 0, 0)),
            pl.BlockSpec((1, m, D_X), lambda b, i: (b, 0, 0)),
            full(w_o),
        ],
        out_specs=pl.BlockSpec((1, tm, d), lambda b, i: (b, i, 0)),
        out_shape=jax.ShapeDtypeStruct((bsz, t, d), F32),
        compiler_params=_cparams(2),
        name="outx",
    )(x, ys, yat, ga, w_out, gx, w_q, qg, kx, vx, w_o)


FFN_CHUNK = 1408


def _ffn_kernel(h_ref, g_ref, wup_ref, cw_ref, cb_ref, wdn_ref, o_ref, tail_ref):
    tm = h_ref.shape[1]

    @pl.when(pl.program_id(1) == 0)
    def _():
        tail_ref[...] = jnp.zeros_like(tail_ref)

    h = h_ref[0]
    hn = _rms_rows(h, g_ref[...]).astype(BF16)
    acc = h

    def conv(up, cols):
        ext = jnp.concatenate([tail_ref[:, cols], up], axis=0)
        tail_ref[:, cols] = up[tm - SUBLANES:]
        w = cw_ref[:, cols]
        return (w[0:1] * ext[SUBLANES - 2:SUBLANES - 2 + tm]
                + w[1:2] * ext[SUBLANES - 1:SUBLANES - 1 + tm]
                + w[2:3] * up + cb_ref[:, cols])

    n_chunks = D_FF // FFN_CHUNK
    cols = [(slice(c * FFN_CHUNK, (c + 1) * FFN_CHUNK),
             slice(D_FF + c * FFN_CHUNK, D_FF + (c + 1) * FFN_CHUNK)) for c in range(n_chunks)]
    ups = [tuple(jnp.dot(hn, wup_ref[:, cs], preferred_element_type=F32) for cs in cols[c])
           for c in range(n_chunks)]
    for c in range(n_chunks):
        act = (jax.nn.silu(conv(ups[c][0], cols[c][0])) * conv(ups[c][1], cols[c][1])).astype(BF16)
        acc = acc + jnp.dot(act, wdn_ref[cols[c][0]], preferred_element_type=F32)
    o_ref[0] = acc


def _ffn(h, g, w_up, conv_w, conv_b, w_down, tm):
    bsz, t, d = h.shape
    full = lambda a: pl.BlockSpec(a.shape, lambda b, i: (0,) * a.ndim, pipeline_mode=pl.Buffered(1))
    return pl.pallas_call(
        _ffn_kernel,
        grid=(bsz, t // tm),
        in_specs=[pl.BlockSpec((1, tm, d), lambda b, i: (b, i, 0)),
                  full(g), full(w_up), full(conv_w), full(conv_b), full(w_down)],
        out_specs=pl.BlockSpec((1, tm, d), lambda b, i: (b, i, 0)),
        out_shape=jax.ShapeDtypeStruct((bsz, t, d), F32),
        scratch_shapes=[pltpu.VMEM((SUBLANES, 2 * D_FF), F32)],
        compiler_params=_cparams(2),
        name="ffn",
    )(h, g, w_up, conv_w, conv_b, w_down)


def _block_diag_b(bb):
    g, c, n = bb.shape
    per_tile = S5_NT // n
    per_slab = LANES // c
    tiles = S5_COLS // S5_NT
    place = np.zeros((tiles, per_slab, per_tile), np.float32)
    for j in range(tiles):
        for q in range(per_tile):
            place[j, (j * per_tile + q) % per_slab, q] = 1.0
    out = jnp.einsum('jqcn,jgq->jgcqn', bb.reshape(tiles, per_tile, c, n), jnp.asarray(place))
    return out.reshape(tiles, LANES, S5_NT).astype(BF16)


def _block_diag_c(cm):
    g, c, n = cm.shape
    per_tile = g // 2
    out = jnp.einsum('tqcn,qp->tqnpc', cm.reshape(2, per_tile, c, n), jnp.eye(per_tile, dtype=F32))
    return out.reshape(2, per_tile * n, per_tile * c).astype(BF16)


def kernel(x, mem, norm_mix, w_in, attn_qn, attn_kn, ssm_lam_re, ssm_lam_im, ssm_log_step, ssm_b_re, ssm_b_im, ssm_c_re, ssm_c_im, ssm_d, ssm_w_glu, ssm_b_glu, gnorm_ssm, gnorm_attn, w_out, rel_bias, norm_xattn, norm_mem, x_wq, x_wkv, x_wo, x_qn, x_kn, norm_ffn, ffn_w_up, ffn_conv_w, ffn_conv_b, ffn_w_down):
    bsz, t, d = x.shape
    assert bsz == SUBLANES and t % (2 * max(INPROJ_PIECE, OUTX_PIECE)) == 0
    assert norm_mix.shape[0] == 1
    l = 0
    row = lambda v: v.reshape(1, -1).astype(F32)

    w = w_in[l]
    w_u = w[:, :D_SSM].astype(BF16)
    w_kqvt = jnp.concatenate([w[:, D_SSM + D_ATTN:D_SSM + 2 * D_ATTN], w[:, D_SSM:D_SSM + D_ATTN],
                              w[:, D_SSM + 2 * D_ATTN:]], axis=1).T.astype(BF16)
    lane_rep = lambda v: jnp.broadcast_to(v[:, None], (v.shape[0], LANES)).astype(F32)
    qg = lane_rep(jnp.tile(attn_qn[l], N_HEADS) * (HEAD_DIM ** -0.5))
    kg = lane_rep(jnp.tile(attn_kn[l], N_HEADS))
    u, k, kmean, qt, vt = _inproj(x, row(norm_mix[l]), w_u, w_kqvt, kg, qg, tm=2 * INPROJ_PIECE)

    a_re, a_im, bb_re, bb_im = _s5prep(ssm_lam_re[l], ssm_lam_im[l], ssm_log_step[l][:, None],
                                       jnp.swapaxes(ssm_b_re[l], 1, 2), jnp.swapaxes(ssm_b_im[l], 1, 2))
    a_re8 = jnp.broadcast_to(a_re.reshape(1, -1), (SUBLANES, S5_COLS))
    a_im8 = jnp.broadcast_to(a_im.reshape(1, -1), (SUBLANES, S5_COLS))
    ys = _s5(u, _block_diag_b(bb_re), _block_diag_b(bb_im), a_re8, a_im8,
             _block_diag_c(ssm_c_re[l]), _block_diag_c(ssm_c_im[l]), row(ssm_d[l]),
             ssm_w_glu[l].astype(BF16), row(ssm_b_glu[l]), row(gnorm_ssm[l]), tt=S5_TIME_TILE)

    own, prev = _bucket_tiles()
    bias_tiles = _biastile(rel_bias, jnp.asarray(own), jnp.asarray(prev))
    nb = t // MOBA_BLOCK
    yat = _moba(rel_bias, qt, k, vt, kmean.reshape(bsz, nb, D_ATTN).astype(BF16), bias_tiles)

    kx, vx = _memkv(mem, row(norm_mem[l]), x_wkv[l].astype(BF16), row(x_kn[l]))
    ga = jnp.broadcast_to(gnorm_attn[l][:, None], (D_ATTN, LANES)).astype(F32)
    h2 = _outx(x, ys, yat, ga, w_out[l].astype(BF16), row(norm_xattn[l]), x_wq[l].astype(BF16),
               row(x_qn[l]), kx, vx, x_wo[l].astype(BF16), tm=2 * OUTX_PIECE)

    return _ffn(h2, row(norm_ffn[l]), ffn_w_up[l].astype(BF16), ffn_conv_w[l].reshape(3, 2 * D_FF),
                row(ffn_conv_b[l]), ffn_w_down[l].astype(BF16), tm=FFN_ROWS)
```

```python
import functools
import math

import numpy as np
import jax
import jax.numpy as jnp
from jax import lax
from jax.experimental import pallas as pl
from jax.experimental.pallas import tpu as pltpu

F32 = jnp.float32
BF16 = jnp.bfloat16

EPS = 1e-6
NEG = -1e30
LOG2E = 1.4426950408889634

D_SSM = 512
N_GROUPS = 32
SSM_STATE = 64
D_ATTN = 512
HEAD_DIM = 64
N_HEADS = 8
MOBA_BLOCK = 256
MOBA_TOPK = 3
MOBA_LOOKAHEAD = 5
MOBA_FAR_GROUP = 8
N_BUCKETS = 32
MAX_DISTANCE = 128
X_HEADS = 4
X_HEAD_DIM = 128
D_X = 512
D_FF = 2816
LANES = 128
SUBLANES = 8
V7X_VMEM_BYTES = 64 * 1024 * 1024
VMEM_LIMIT = V7X_VMEM_BYTES * 7 // 8

S5_TIME_TILE = 128
FFN_ROWS = 512


def _cparams(n_axes):
    return pltpu.CompilerParams(
        dimension_semantics=("arbitrary",) * n_axes, vmem_limit_bytes=VMEM_LIMIT)


def _rms_rows(x, g):
    ms = jnp.mean(x * x, axis=-1, keepdims=True)
    return x * lax.rsqrt(ms + EPS) * g


INPROJ_PIECE = 512


def _inproj_kernel(x_ref, g_ref, wu_ref, wkqvt_ref, kg_ref, qg_ref,
                   u_ref, k_ref, kmean_ref, qt_ref, vt_ref):
    tp = INPROJ_PIECE
    pieces = range(x_ref.shape[1] // tp)
    per_piece = tp // MOBA_BLOCK
    rows = [slice(r * tp, (r + 1) * tp) for r in pieces]
    hn = [_rms_rows(x_ref[0, rows[r]], g_ref[...]).astype(BF16) for r in pieces]
    p_u, p_t = [], []
    for r in pieces:
        p_u.append(jnp.dot(hn[r], wu_ref[...], preferred_element_type=F32))
        p_t.append(lax.dot_general(wkqvt_ref[...], hn[r], (((1,), (1,)), ((), ())),
                                   preferred_element_type=F32))

    def head_norm_t(xt, gain_ref):
        x3 = xt.reshape(N_HEADS, HEAD_DIM, tp)
        ms = jnp.mean(x3 * x3, axis=1, keepdims=True)
        return (x3 * lax.rsqrt(ms + EPS)).reshape(D_ATTN, tp) * jnp.tile(gain_ref[...], (1, tp // LANES))

    for r in pieces:
        u_ref[0, rows[r]] = p_u[r]
        kn = head_norm_t(p_t[r][:D_ATTN], kg_ref).T
        k_ref[0, rows[r]] = (kn * LOG2E).astype(BF16)
        qn = head_norm_t(p_t[r][D_ATTN:2 * D_ATTN], qg_ref)
        vt = p_t[r][2 * D_ATTN:]
        for b in range(per_piece):
            cs = slice(b * MOBA_BLOCK, (b + 1) * MOBA_BLOCK)
            qt_ref[0, r * per_piece + b] = qn[:, cs].astype(BF16)
            vt_ref[0, r * per_piece + b] = vt[:, cs].astype(BF16)
            kmean_ref[0, r * per_piece + b] = jnp.mean(kn[cs], axis=0, keepdims=True)


def _inproj(x, g, w_u, w_kqvt, kg, qg, tm):
    bsz, t, d = x.shape
    nb = t // MOBA_BLOCK
    rb = tm // MOBA_BLOCK
    full = lambda shp: pl.BlockSpec(shp, lambda b, i: (0,) * len(shp))
    return pl.pallas_call(
        _inproj_kernel,
        grid=(bsz, t // tm),
        in_specs=[
            pl.BlockSpec((1, tm, d), lambda b, i: (b, i, 0)),
            full((1, d)), full(w_u.shape), full(w_kqvt.shape), full(kg.shape), full(qg.shape),
        ],
        out_specs=[
            pl.BlockSpec((1, tm, D_SSM), lambda b, i: (b, i, 0)),
            pl.BlockSpec((1, tm, D_ATTN), lambda b, i: (b, i, 0)),
            pl.BlockSpec((1, rb, 1, D_ATTN), lambda b, i: (b, i, 0, 0)),
            pl.BlockSpec((1, rb, D_ATTN, MOBA_BLOCK), lambda b, i: (b, i, 0, 0)),
            pl.BlockSpec((1, rb, D_ATTN, MOBA_BLOCK), lambda b, i: (b, i, 0, 0)),
        ],
        out_shape=[
            jax.ShapeDtypeStruct((bsz, t, D_SSM), F32),
            jax.ShapeDtypeStruct((bsz, t, D_ATTN), BF16),
            jax.ShapeDtypeStruct((bsz, nb, 1, D_ATTN), F32),
            jax.ShapeDtypeStruct((bsz, nb, D_ATTN, MOBA_BLOCK), BF16),
            jax.ShapeDtypeStruct((bsz, nb, D_ATTN, MOBA_BLOCK), BF16),
        ],
        compiler_params=_cparams(2),
        name="inproj",
    )(x, g, w_u, w_kqvt, kg, qg)


def _s5prep_kernel(lre_ref, lim_ref, ls_ref, bre_ref, bim_ref, are_ref, aim_ref, bbre_ref, bbim_ref):
    a_re = jnp.minimum(lre_ref[...], -1e-4)
    a_im = lim_ref[...]
    step = jnp.exp(ls_ref[...])
    mag = jnp.exp(step * a_re)
    ab_re = mag * jnp.cos(step * a_im)
    ab_im = mag * jnp.sin(step * a_im)
    den = a_re * a_re + a_im * a_im
    p = ab_re - 1.0
    f_re = (p * a_re + ab_im * a_im) / den
    f_im = (ab_im * a_re - p * a_im) / den
    are_ref[...] = ab_re
    aim_ref[...] = ab_im
    bbre_ref[...] = f_re[:, None, :] * bre_ref[...] - f_im[:, None, :] * bim_ref[...]
    bbim_ref[...] = f_re[:, None, :] * bim_ref[...] + f_im[:, None, :] * bre_ref[...]


def _s5prep(lam_re, lam_im, log_step, b_re_t, b_im_t):
    g, n = lam_re.shape
    c = b_re_t.shape[1]
    return pl.pallas_call(
        _s5prep_kernel,
        out_shape=[jax.ShapeDtypeStruct((g, n), F32), jax.ShapeDtypeStruct((g, n), F32),
                   jax.ShapeDtypeStruct((g, c, n), F32), jax.ShapeDtypeStruct((g, c, n), F32)],
        name="s5prep",
    )(lam_re, lam_im, log_step, b_re_t, b_im_t)


S5_COLS = N_GROUPS * SSM_STATE
S5_NT = 256
S5_SCAN_COLS = 512
S5_PARTS = 4


def _s5_kernel(u_ref, bwre_ref, bwim_ref, are_ref, aim_ref, cwre_ref, cwim_ref, d_ref, wglu_ref,
               bglu_ref, gn_ref, o_ref, sre_ref, sim_ref, stre_ref, stim_ref, ut_ref, yt_ref):
    bsz, tt, _ = u_ref.shape
    n_slab = D_SSM // LANES

    @pl.when(pl.program_id(0) == 0)
    def _():
        stre_ref[...] = jnp.zeros_like(stre_ref)
        stim_ref[...] = jnp.zeros_like(stim_ref)

    for b in range(bsz):
        for s in range(n_slab):
            ut_ref[s, pl.ds(b, tt, stride=bsz), :] = u_ref[b, :, s * LANES:(s + 1) * LANES]
    n_parts = S5_PARTS
    tq = tt // n_parts
    n_chunks = S5_COLS // S5_SCAN_COLS
    chunks = [slice(cc * S5_SCAN_COLS, (cc + 1) * S5_SCAN_COLS) for cc in range(n_chunks)]
    a_re = [are_ref[:, cs] for cs in chunks]
    a_im = [aim_ref[:, cs] for cs in chunks]
    state = [(stre_ref[:, cs], stim_ref[:, cs]) for cs in chunks]
    half = S5_COLS // 2

    def part_rows(q):
        return slice(q * tq * bsz, (q + 1) * tq * bsz)

    def bu(q):
        rs = part_rows(q)
        slabs = [ut_ref[s, rs, :].astype(BF16) for s in range(n_slab)]
        for j in range(S5_COLS // S5_NT):
            cs = slice(j * S5_NT, (j + 1) * S5_NT)
            sre_ref[rs, cs] = jnp.dot(slabs[j // 2], bwre_ref[j], preferred_element_type=F32)
            sim_ref[rs, cs] = jnp.dot(slabs[j // 2], bwim_ref[j], preferred_element_type=F32)

    def scan(q):
        for cc, cs in enumerate(chunks):
            s_re, s_im = state[cc]
            for t in range(q * tq, (q + 1) * tq):
                rs = slice(t * bsz, (t + 1) * bsz)
                n_re = a_re[cc] * s_re - a_im[cc] * s_im + sre_ref[rs, cs]
                n_im = a_re[cc] * s_im + a_im[cc] * s_re + sim_ref[rs, cs]
                sre_ref[rs, cs] = n_re
                sim_ref[rs, cs] = n_im
                s_re, s_im = n_re, n_im
            state[cc] = (s_re, s_im)

    y_parts = {}

    def cs_out(q):
        rs = part_rows(q)
        ys = []
        for n in range(2):
            ks = slice(n * half, (n + 1) * half)
            ys.append(jnp.dot(sre_ref[rs, ks].astype(BF16), cwre_ref[n], preferred_element_type=F32)
                      - jnp.dot(sim_ref[rs, ks].astype(BF16), cwim_ref[n], preferred_element_type=F32))
        u = jnp.concatenate([ut_ref[s, rs, :] for s in range(n_slab)], axis=1)
        y_parts[q] = jax.nn.gelu(jnp.concatenate(ys, axis=1) + d_ref[...] * u)

    def glu(q):
        rs = part_rows(q)
        y = y_parts.pop(q)
        z = jnp.dot(y.astype(BF16), wglu_ref[...], preferred_element_type=F32) + bglu_ref[...]
        yn = _rms_rows(y * jax.nn.sigmoid(z), gn_ref[...])
        for s in range(n_slab):
            yt_ref[s, rs, :] = yn[:, s * LANES:(s + 1) * LANES]

    bu(0)
    for q in range(n_parts + 3):
        if q + 1 < n_parts:
            bu(q + 1)
        if q < n_parts:
            scan(q)
        if 0 <= q - 1 < n_parts:
            cs_out(q - 1)
        if 0 <= q - 2 < n_parts:
            glu(q - 2)
    for cc, cs in enumerate(chunks):
        stre_ref[:, cs], stim_ref[:, cs] = state[cc]

    for b in range(bsz):
        for s in range(n_slab):
            o_ref[b, :, s * LANES:(s + 1) * LANES] = yt_ref[s, pl.ds(b, tt, stride=bsz), :].astype(o_ref.dtype)


def _s5(u, bw_re, bw_im, a_re8, a_im8, cw_re, cw_im, d, w_glu, b_glu, gn, tt):
    bsz, t, _ = u.shape
    assert bsz == SUBLANES
    rows = tt * bsz
    full = lambda a: pl.BlockSpec(a.shape, lambda i: (0,) * a.ndim)
    consts = (bw_re, bw_im, a_re8, a_im8, cw_re, cw_im, d, w_glu, b_glu, gn)
    slab_rows = pltpu.VMEM((D_SSM // LANES, rows, LANES), F32)
    return pl.pallas_call(
        _s5_kernel,
        grid=(t // tt,),
        in_specs=[pl.BlockSpec((bsz, tt, D_SSM), lambda i: (0, i, 0))] + [full(a) for a in consts],
        out_specs=pl.BlockSpec((bsz, tt, D_SSM), lambda i: (0, i, 0)),
        out_shape=jax.ShapeDtypeStruct((bsz, t, D_SSM), BF16),
        scratch_shapes=[pltpu.VMEM((rows, S5_COLS), F32), pltpu.VMEM((rows, S5_COLS), F32),
                        pltpu.VMEM((bsz, S5_COLS), F32), pltpu.VMEM((bsz, S5_COLS), F32),
                        slab_rows, slab_rows],
        compiler_params=_cparams(1),
        name="s5",
    )(u, *consts)


def _t5_bucket_table(n):
    d = np.arange(n)
    max_exact = N_BUCKETS // 2
    nf = np.maximum(d, max_exact).astype(np.float32)
    large = max_exact + (np.log(nf / max_exact) / math.log(MAX_DISTANCE / max_exact)
                         * (N_BUCKETS - max_exact)).astype(np.int32)
    large = np.minimum(large, N_BUCKETS - 1)
    return np.where(d < max_exact, d, large).astype(np.int32)


def _bucket_tiles():
    tbl = _t5_bucket_table(2 * MOBA_BLOCK)
    r = np.arange(MOBA_BLOCK)[:, None]
    c = np.arange(MOBA_BLOCK)[None, :]
    d_own = c - r
    own = np.where(d_own >= 0, tbl[np.maximum(d_own, 0)], -1).astype(np.int32)
    prev = tbl[d_own + MOBA_BLOCK].astype(np.int32)
    return own, prev


def _biastile_kernel(rb_ref, own_ref, prev_ref, o_ref):
    h = pl.program_id(0)
    own = own_ref[...]
    prev = prev_ref[...]
    t_own = jnp.full(own.shape, NEG, F32)
    t_prev = jnp.zeros(prev.shape, F32)
    for b in range(N_BUCKETS):
        v = rb_ref[b, h] * LOG2E
        t_own = jnp.where(own == b, v, t_own)
        t_prev = jnp.where(prev == b, v, t_prev)
    o_ref[0, 0] = t_own
    o_ref[0, 1] = t_prev


def _biastile(rel_bias, own, prev):
    blk = own.shape
    return pl.pallas_call(
        _biastile_kernel,
        grid=(N_HEADS,),
        in_specs=[pl.BlockSpec(memory_space=pltpu.SMEM),
                  pl.BlockSpec(blk, lambda h: (0, 0)), pl.BlockSpec(blk, lambda h: (0, 0))],
        out_specs=pl.BlockSpec((1, 2) + blk, lambda h: (h, 0, 0, 0)),
        out_shape=jax.ShapeDtypeStruct((N_HEADS, 2) + blk, F32),
        compiler_params=_cparams(1),
        name="biastile",
    )(rel_bias, own, prev)


MOBA_BIAS_PARTS = 3
MOBA_ONES_ROWS = 16
MOBA_EXTRA_ROWS = 32


def _moba_selector(nb):
    e = np.zeros((nb, MOBA_BLOCK, LANES), np.float32)
    for j in range(nb):
        e[j, :, j] = 1.0
        e[j, :, nb:nb + MOBA_BIAS_PARTS] = 1.0
    return e


def _moba_kernel(rb_ref, qt_ref, k_ref, vt_ref, kmean_ref, bias_ref, esel_ref, o_ref,
                 qpad_ref, sel_ref, m_ref, acc_ref):
    i = pl.program_id(1)
    nb = kmean_ref.shape[1]
    tq = qt_ref.shape[3]
    pair_w = 2 * HEAD_DIM
    heads = range(N_HEADS)

    def pair_cols(h):
        return slice((h // 2) * pair_w, (h // 2 + 1) * pair_w)

    blk = lax.broadcasted_iota(jnp.int32, (nb, tq), 0)
    for h in heads:
        qt = qt_ref[0, 0, h * HEAD_DIM:(h + 1) * HEAD_DIM, :]
        zero = jnp.zeros_like(qt)
        qpad = jnp.concatenate([qt, zero] if h % 2 == 0 else [zero, qt], axis=0)
        gate = jnp.dot(kmean_ref[0, :, pair_cols(h)], qpad, preferred_element_type=F32)
        g = jnp.where(blk < i, gate, -jnp.inf)
        sel = jnp.zeros((nb, tq), F32)
        for _ in range(MOBA_TOPK):
            mx = jnp.max(g, axis=0, keepdims=True)
            cand = jnp.where((g == mx) & (mx > -jnp.inf), blk, nb)
            idx = jnp.min(cand, axis=0, keepdims=True)
            pick = blk == idx
            sel = jnp.where(pick, 1.0, sel)
            g = jnp.where(pick, -jnp.inf, g)
        mask = jnp.where(sel > 0.0, 0.0, NEG)
        sel_ref[h] = mask
        c = jnp.full((nb, tq), rb_ref[N_BUCKETS - 1, h] * LOG2E, F32)
        cb = jnp.zeros((nb, tq), F32)
        for r in range(MOBA_BIAS_PARTS):
            pc = c.astype(BF16).astype(F32)
            cb = jnp.where(blk == r, pc, cb)
            c = c - pc
        pad = jnp.zeros((pair_w - 2 * nb, tq), BF16)
        qpad_ref[h] = jnp.concatenate([qpad, mask.astype(BF16), cb.astype(BF16), pad], axis=0)

    def key_block(h, j):
        r0 = pl.multiple_of(j * MOBA_BLOCK, MOBA_BLOCK)
        return k_ref[0, pl.ds(r0, MOBA_BLOCK), pair_cols(h)]

    def near_scores(h, j, bias):
        return jnp.dot(key_block(h, j), qpad_ref[h, 0:pair_w], preferred_element_type=F32) + bias

    def far_scores(h, j):
        lhs = jnp.concatenate([key_block(h, j), esel_ref[j, :, 0:MOBA_EXTRA_ROWS]], axis=1)
        return jnp.dot(lhs, qpad_ref[h, 0:pair_w + MOBA_EXTRA_ROWS],
                       preferred_element_type=F32)

    ones = jnp.ones((MOBA_ONES_ROWS, MOBA_BLOCK), BF16)

    def tile_pass(tiles, state):
        queue = {}
        for t in range(min(MOBA_LOOKAHEAD, len(tiles))):
            queue[t] = tiles[t][2]()
        for t in range(len(tiles)):
            if t + MOBA_LOOKAHEAD < len(tiles):
                queue[t + MOBA_LOOKAHEAD] = tiles[t + MOBA_LOOKAHEAD][2]()
            h, j, _ = tiles[t]
            s = queue.pop(t)
            m_new = jnp.max(s, axis=0, keepdims=True)
            if h in state:
                m_new = jnp.maximum(state[h][0], m_new)
            p = jnp.exp2(s - m_new).astype(BF16)
            v1 = jnp.concatenate([vt_ref[0, j, h * HEAD_DIM:(h + 1) * HEAD_DIM, :], ones], axis=0)
            acc_new = jnp.dot(v1, p, preferred_element_type=F32)
            if h in state:
                acc_new = jnp.exp2(state[h][0] - m_new) * state[h][1] + acc_new
            state[h] = (m_new, acc_new)
        return state

    def load_state():
        return {h: (m_ref[h], acc_ref[h]) for h in heads}

    def store_state(state):
        for h in heads:
            m_ref[h], acc_ref[h] = state[h]

    def own_tiles():
        return [(h, i, functools.partial(near_scores, h, i, bias_ref[h, 0])) for h in heads]

    @pl.when(i == 0)
    def _():
        store_state(tile_pass(own_tiles(), {}))

    def far_tiles(j0, count):
        return [(h, j0 + jj, functools.partial(far_scores, h, j0 + jj)) for jj in range(count) for h in heads]

    n_far = jnp.maximum(i - 1, 0)
    rem = n_far % MOBA_FAR_GROUP
    for r in range(MOBA_FAR_GROUP):
        @pl.when((i >= 1) & (rem == r))
        def _(r=r):
            j = i - 1
            prev = [(h, j, functools.partial(near_scores, h, j, bias_ref[h, 1] + sel_ref[h, pl.ds(j, 1), :]))
                    for h in heads]
            store_state(tile_pass(own_tiles() + prev + far_tiles(0, r), {}))

    def far_group(it, flat):
        state = tile_pass(far_tiles(rem + MOBA_FAR_GROUP * it, MOBA_FAR_GROUP), {h: flat[h] for h in heads})
        return tuple(state[h] for h in heads)

    state0 = load_state()
    final = lax.fori_loop(0, n_far // MOBA_FAR_GROUP, far_group, tuple(state0[h] for h in heads))

    for h in heads:
        acc = final[h][1]
        o_ref[0, 0, h * HEAD_DIM:(h + 1) * HEAD_DIM, :] = (
            acc[:HEAD_DIM] / acc[HEAD_DIM:HEAD_DIM + 1]).astype(o_ref.dtype)


def _moba(rel_bias, qt, k, vt, kmean, bias_tiles):
    bsz, nb, _, tq = qt.shape
    t = k.shape[1]
    assert nb + MOBA_BIAS_PARTS <= MOBA_EXTRA_ROWS
    esel = jnp.asarray(_moba_selector(nb), BF16)
    once = lambda a: pl.BlockSpec(a.shape, lambda b, i: (0,) * a.ndim, pipeline_mode=pl.Buffered(1))
    return pl.pallas_call(
        _moba_kernel,
        grid=(bsz, nb),
        in_specs=[
            pl.BlockSpec(memory_space=pltpu.SMEM),
            pl.BlockSpec((1, 1, D_ATTN, tq), lambda b, i: (b, i, 0, 0)),
            pl.BlockSpec((1, t, D_ATTN), lambda b, i: (b, 0, 0)),
            pl.BlockSpec((1, nb, D_ATTN, tq), lambda b, i: (b, 0, 0, 0)),
            pl.BlockSpec((1, nb, D_ATTN), lambda b, i: (b, 0, 0)),
            once(bias_tiles), once(esel),
        ],
        out_specs=pl.BlockSpec((1, 1, D_ATTN, tq), lambda b, i: (b, i, 0, 0)),
        out_shape=jax.ShapeDtypeStruct((bsz, nb, D_ATTN, tq), BF16),
        scratch_shapes=[pltpu.VMEM((N_HEADS, 4 * HEAD_DIM, tq), BF16), pltpu.VMEM((N_HEADS, nb, tq), F32),
                        pltpu.VMEM((N_HEADS, 1, tq), F32),
                        pltpu.VMEM((N_HEADS, HEAD_DIM + MOBA_ONES_ROWS, tq), F32)],
        compiler_params=_cparams(2),
        name="moba",
    )(rel_bias, qt, k, vt, kmean, bias_tiles, esel)


def _memkv_kernel(mem_ref, g_ref, wkv_ref, kg_ref, k_ref, v_ref):
    mn = _rms_rows(mem_ref[0], g_ref[...]).astype(BF16)
    kv = jnp.dot(mn, wkv_ref[...], preferred_element_type=F32)
    ks = []
    for hh in range(X_HEADS):
        cs = slice(hh * X_HEAD_DIM, (hh + 1) * X_HEAD_DIM)
        ks.append(_rms_rows(kv[:, cs], kg_ref[...]))
    k_ref[0] = jnp.concatenate(ks, axis=1).astype(BF16)
    v_ref[0] = kv[:, D_X:].astype(BF16)


def _memkv(mem, g, w_kv, kg):
    bsz, m, d = mem.shape
    full = lambda a: pl.BlockSpec(a.shape, lambda b: (0,) * a.ndim)
    return pl.pallas_call(
        _memkv_kernel,
        grid=(bsz,),
        in_specs=[pl.BlockSpec((1, m, d), lambda b: (b, 0, 0)), full(g), full(w_kv), full(kg)],
        out_specs=[pl.BlockSpec((1, m, D_X), lambda b: (b, 0, 0))] * 2,
        out_shape=[jax.ShapeDtypeStruct((bsz, m, D_X), BF16)] * 2,
        compiler_params=_cparams(1),
        name="memkv",
    )(mem, g, w_kv, kg)


OUTX_PIECE = 512


def _outx_kernel(x_ref, ys_ref, yat_ref, ga_ref, wout_ref, gx_ref, wq_ref, qg_ref, kx_ref, vx_ref,
                 wo_ref, o_ref):
    per_piece = OUTX_PIECE // MOBA_BLOCK
    pieces = range(yat_ref.shape[1] // per_piece)
    rows = [slice(r * OUTX_PIECE, (r + 1) * OUTX_PIECE) for r in pieces]
    ya_n = []
    for r in pieces:
        parts = []
        for blk in range(r * per_piece, (r + 1) * per_piece):
            ya = yat_ref[0, blk].astype(F32)
            ms = jnp.mean(ya * ya, axis=0, keepdims=True)
            yan = ya * lax.rsqrt(ms + EPS) * jnp.tile(ga_ref[...], (1, ya.shape[1] // LANES))
            parts.append(yan.T)
        ya_n.append(jnp.concatenate(parts, axis=0).astype(BF16))
    h1 = [x_ref[0, rows[r]]
          + jnp.dot(ys_ref[0, rows[r]], wout_ref[:D_SSM], preferred_element_type=F32)
          + jnp.dot(ya_n[r], wout_ref[D_SSM:], preferred_element_type=F32) for r in pieces]
    hn = [_rms_rows(h1[r], gx_ref[...]).astype(BF16) for r in pieces]
    q = [jnp.dot(hn[r], wq_ref[...], preferred_element_type=F32) for r in pieces]
    outs = [[] for _ in pieces]
    head_cols = [slice(hh * X_HEAD_DIM, (hh + 1) * X_HEAD_DIM) for hh in range(X_HEADS)]

    def scores(hh):
        cs = head_cols[hh]
        qh = [_rms_rows(q[r][:, cs], qg_ref[...]).astype(BF16) for r in pieces]
        return [lax.dot_general(qh[r], kx_ref[0, :, cs], (((1,), (1,)), ((), ())),
                                preferred_element_type=F32) * (X_HEAD_DIM ** -0.5) for r in pieces]

    s_next = scores(0)
    for hh in range(X_HEADS):
        s = s_next
        if hh + 1 < X_HEADS:
            s_next = scores(hh + 1)
        for r in pieces:
            e = jnp.exp(s[r] - jnp.max(s[r], axis=-1, keepdims=True))
            prob = e / jnp.sum(e, axis=-1, keepdims=True)
            outs[r].append(jnp.dot(prob.astype(BF16), vx_ref[0, :, head_cols[hh]],
                                   preferred_element_type=F32))
    for r in pieces:
        o = jnp.concatenate(outs[r], axis=1).astype(BF16)
        o_ref[0, rows[r]] = h1[r] + jnp.dot(o, wo_ref[...], preferred_element_type=F32)


def _outx(x, ys, yat, ga, w_out, gx, w_q, qg, kx, vx, w_o, tm):
    bsz, t, d = x.shape
    m = kx.shape[1]
    rb = tm // MOBA_BLOCK
    full = lambda a: pl.BlockSpec(a.shape, lambda b, i: (0,) * a.ndim)
    return pl.pallas_call(
        _outx_kernel,
        grid=(bsz, t // tm),
        in_specs=[
            pl.BlockSpec((1, tm, d), lambda b, i: (b, i, 0)),
            pl.BlockSpec((1, tm, D_SSM), lambda b, i: (b, i, 0)),
            pl.BlockSpec((1, rb, D_ATTN, MOBA_BLOCK), lambda b, i: (b, i, 0, 0)),
            full(ga), full(w_out), full(gx), full(w_q), full(qg),
            pl.BlockSpec((1, m, D_X), lambda b, i: (b, 0, 0)),
            pl.BlockSpec((1, m, D_X), lambda b, i: (b, 0, 0)),
            full(w_o),
        ],
        out_specs=pl.BlockSpec((1, tm, d), lambda b, i: (b, i, 0)),
        out_shape=jax.ShapeDtypeStruct((bsz, t, d), F32),
        compiler_params=_cparams(2),
        name="outx",
    )(x, ys, yat, ga, w_out, gx, w_q, qg, kx, vx, w_o)


FFN_CHUNK = 1408


def _ffn_kernel(h_ref, g_ref, wup_ref, cw_ref, cb_ref, wdn_ref, o_ref, tail_ref):
    tm = h_ref.shape[1]

    @pl.when(pl.program_id(1) == 0)
    def _():
        tail_ref[...] = jnp.zeros_like(tail_ref)

    h = h_ref[0]
    hn = _rms_rows(h, g_ref[...]).astype(BF16)
    acc = h

    def conv(up, cols):
        ext = jnp.concatenate([tail_ref[:, cols], up], axis=0)
        tail_ref[:, cols] = up[tm - SUBLANES:]
        w = cw_ref[:, cols]
        return (w[0:1] * ext[SUBLANES - 2:SUBLANES - 2 + tm]
                + w[1:2] * ext[SUBLANES - 1:SUBLANES - 1 + tm]
                + w[2:3] * up + cb_ref[:, cols])

    n_chunks = D_FF // FFN_CHUNK
    cols = [(slice(c * FFN_CHUNK, (c + 1) * FFN_CHUNK),
             slice(D_FF + c * FFN_CHUNK, D_FF + (c + 1) * FFN_CHUNK)) for c in range(n_chunks)]
    ups = [tuple(jnp.dot(hn, wup_ref[:, cs], preferred_element_type=F32) for cs in cols[c])
           for c in range(n_chunks)]
    for c in range(n_chunks):
        act = (jax.nn.silu(conv(ups[c][0], cols[c][0])) * conv(ups[c][1], cols[c][1])).astype(BF16)
        acc = acc + jnp.dot(act, wdn_ref[cols[c][0]], preferred_element_type=F32)
    o_ref[0] = acc


def _ffn(h, g, w_up, conv_w, conv_b, w_down, tm):
    bsz, t, d = h.shape
    full = lambda a: pl.BlockSpec(a.shape, lambda b, i: (0,) * a.ndim, pipeline_mode=pl.Buffered(1))
    return pl.pallas_call(
        _ffn_kernel,
        grid=(bsz, t // tm),
        in_specs=[pl.BlockSpec((1, tm, d), lambda b, i: (b, i, 0)),
                  full(g), full(w_up), full(conv_w), full(conv_b), full(w_down)],
        out_specs=pl.BlockSpec((1, tm, d), lambda b, i: (b, i, 0)),
        out_shape=jax.ShapeDtypeStruct((bsz, t, d), F32),
        scratch_shapes=[pltpu.VMEM((SUBLANES, 2 * D_FF), F32)],
        compiler_params=_cparams(2),
        name="ffn",
    )(h, g, w_up, conv_w, conv_b, w_down)


def _block_diag_b(bb):
    g, c, n = bb.shape
    per_tile = S5_NT // n
    per_slab = LANES // c
    tiles = S5_COLS // S5_NT
    place = np.zeros((tiles, per_slab, per_tile), np.float32)
    for j in range(tiles):
        for q in range(per_tile):
            place[j, (j * per_tile + q) % per_slab, q] = 1.0
    out = jnp.einsum('jqcn,jgq->jgcqn', bb.reshape(tiles, per_tile, c, n), jnp.asarray(place))
    return out.reshape(tiles, LANES, S5_NT).astype(BF16)


def _block_diag_c(cm):
    g, c, n = cm.shape
    per_tile = g // 2
    out = jnp.einsum('tqcn,qp->tqnpc', cm.reshape(2, per_tile, c, n), jnp.eye(per_tile, dtype=F32))
    return out.reshape(2, per_tile * n, per_tile * c).astype(BF16)


def kernel(x, mem, norm_mix, w_in, attn_qn, attn_kn, ssm_lam_re, ssm_lam_im, ssm_log_step, ssm_b_re, ssm_b_im, ssm_c_re, ssm_c_im, ssm_d, ssm_w_glu, ssm_b_glu, gnorm_ssm, gnorm_attn, w_out, rel_bias, norm_xattn, norm_mem, x_wq, x_wkv, x_wo, x_qn, x_kn, norm_ffn, ffn_w_up, ffn_conv_w, ffn_conv_b, ffn_w_down):
    bsz, t, d = x.shape
    assert bsz == SUBLANES and t % (2 * max(INPROJ_PIECE, OUTX_PIECE)) == 0
    assert norm_mix.shape[0] == 1
    l = 0
    row = lambda v: v.reshape(1, -1).astype(F32)

    w = w_in[l]
    w_u = w[:, :D_SSM].astype(BF16)
    w_kqvt = jnp.concatenate([w[:, D_SSM + D_ATTN:D_SSM + 2 * D_ATTN], w[:, D_SSM:D_SSM + D_ATTN],
                              w[:, D_SSM + 2 * D_ATTN:]], axis=1).T.astype(BF16)
    lane_rep = lambda v: jnp.broadcast_to(v[:, None], (v.shape[0], LANES)).astype(F32)
    qg = lane_rep(jnp.tile(attn_qn[l], N_HEADS) * (HEAD_DIM ** -0.5))
    kg = lane_rep(jnp.tile(attn_kn[l], N_HEADS))
    u, k, kmean, qt, vt = _inproj(x, row(norm_mix[l]), w_u, w_kqvt, kg, qg, tm=2 * INPROJ_PIECE)

    a_re, a_im, bb_re, bb_im = _s5prep(ssm_lam_re[l], ssm_lam_im[l], ssm_log_step[l][:, None],
                                       jnp.swapaxes(ssm_b_re[l], 1, 2), jnp.swapaxes(ssm_b_im[l], 1, 2))
    a_re8 = jnp.broadcast_to(a_re.reshape(1, -1), (SUBLANES, S5_COLS))
    a_im8 = jnp.broadcast_to(a_im.reshape(1, -1), (SUBLANES, S5_COLS))
    ys = _s5(u, _block_diag_b(bb_re), _block_diag_b(bb_im), a_re8, a_im8,
             _block_diag_c(ssm_c_re[l]), _block_diag_c(ssm_c_im[l]), row(ssm_d[l]),
             ssm_w_glu[l].astype(BF16), row(ssm_b_glu[l]), row(gnorm_ssm[l]), tt=S5_TIME_TILE)

    own, prev = _bucket_tiles()
    bias_tiles = _biastile(rel_bias, jnp.asarray(own), jnp.asarray(prev))
    nb = t // MOBA_BLOCK
    yat = _moba(rel_bias, qt, k, vt, kmean.reshape(bsz, nb, D_ATTN).astype(BF16), bias_tiles)

    kx, vx = _memkv(mem, row(norm_mem[l]), x_wkv[l].astype(BF16), row(x_kn[l]))
    ga = jnp.broadcast_to(gnorm_attn[l][:, None], (D_ATTN, LANES)).astype(F32)
    h2 = _outx(x, ys, yat, ga, w_out[l].astype(BF16), row(norm_xattn[l]), x_wq[l].astype(BF16),
               row(x_qn[l]), kx, vx, x_wo[l].astype(BF16), tm=2 * OUTX_PIECE)

    return _ffn(h2, row(norm_ffn[l]), ffn_w_up[l].astype(BF16), ffn_conv_w[l].reshape(3, 2 * D_FF),
                row(ffn_conv_b[l]), ffn_w_down[l].astype(BF16), tm=FFN_ROWS)
```

```python
import functools
import math

import numpy as np
import jax
import jax.numpy as jnp
from jax import lax
from jax.experimental import pallas as pl
from jax.experimental.pallas import tpu as pltpu

F32 = jnp.float32
BF16 = jnp.bfloat16

EPS = 1e-6
NEG = -1e30
LOG2E = 1.4426950408889634

D_SSM = 512
N_GROUPS = 32
SSM_STATE = 64
D_ATTN = 512
HEAD_DIM = 64
N_HEADS = 8
MOBA_BLOCK = 256
MOBA_TOPK = 3
MOBA_LOOKAHEAD = 6
MOBA_FAR_GROUP = 8
N_BUCKETS = 32
MAX_DISTANCE = 128
X_HEADS = 4
X_HEAD_DIM = 128
D_X = 512
D_FF = 2816
LANES = 128
SUBLANES = 8
V7X_VMEM_BYTES = 64 * 1024 * 1024
VMEM_LIMIT = V7X_VMEM_BYTES * 7 // 8

S5_TIME_TILE = 128
FFN_ROWS = 512


def _cparams(n_axes):
    return pltpu.CompilerParams(
        dimension_semantics=("arbitrary",) * n_axes, vmem_limit_bytes=VMEM_LIMIT)


def _rms_rows(x, g):
    ms = jnp.mean(x * x, axis=-1, keepdims=True)
    return x * lax.rsqrt(ms + EPS) * g


INPROJ_PIECE = 512


def _inproj_kernel(x_ref, g_ref, wu_ref, wkqvt_ref, kg_ref, qg_ref,
                   u_ref, k_ref, kmean_ref, qt_ref, vt_ref):
    tp = INPROJ_PIECE
    pieces = range(x_ref.shape[1] // tp)
    per_piece = tp // MOBA_BLOCK
    rows = [slice(r * tp, (r + 1) * tp) for r in pieces]
    hn = [_rms_rows(x_ref[0, rows[r]], g_ref[...]).astype(BF16) for r in pieces]
    p_u, p_t = [], []
    for r in pieces:
        p_u.append(jnp.dot(hn[r], wu_ref[...], preferred_element_type=F32))
        p_t.append(lax.dot_general(wkqvt_ref[...], hn[r], (((1,), (1,)), ((), ())),
                                   preferred_element_type=F32))

    def head_norm_t(xt, gain_ref):
        x3 = xt.reshape(N_HEADS, HEAD_DIM, tp)
        ms = jnp.mean(x3 * x3, axis=1, keepdims=True)
        return (x3 * lax.rsqrt(ms + EPS)).reshape(D_ATTN, tp) * jnp.tile(gain_ref[...], (1, tp // LANES))

    for r in pieces:
        u_ref[0, rows[r]] = p_u[r]
        kn = head_norm_t(p_t[r][:D_ATTN], kg_ref).T
        k_ref[0, rows[r]] = (kn * LOG2E).astype(BF16)
        qn = head_norm_t(p_t[r][D_ATTN:2 * D_ATTN], qg_ref)
        vt = p_t[r][2 * D_ATTN:]
        for b in range(per_piece):
            cs = slice(b * MOBA_BLOCK, (b + 1) * MOBA_BLOCK)
            qt_ref[0, r * per_piece + b] = qn[:, cs].astype(BF16)
            vt_ref[0, r * per_piece + b] = vt[:, cs].astype(BF16)
            kmean_ref[0, r * per_piece + b] = jnp.mean(kn[cs], axis=0, keepdims=True)


def _inproj(x, g, w_u, w_kqvt, kg, qg, tm):
    bsz, t, d = x.shape
    nb = t // MOBA_BLOCK
    rb = tm // MOBA_BLOCK
    full = lambda shp: pl.BlockSpec(shp, lambda b, i: (0,) * len(shp))
    return pl.pallas_call(
        _inproj_kernel,
        grid=(bsz, t // tm),
        in_specs=[
            pl.BlockSpec((1, tm, d), lambda b, i: (b, i, 0)),
            full((1, d)), full(w_u.shape), full(w_kqvt.shape), full(kg.shape), full(qg.shape),
        ],
        out_specs=[
            pl.BlockSpec((1, tm, D_SSM), lambda b, i: (b, i, 0)),
            pl.BlockSpec((1, tm, D_ATTN), lambda b, i: (b, i, 0)),
            pl.BlockSpec((1, rb, 1, D_ATTN), lambda b, i: (b, i, 0, 0)),
            pl.BlockSpec((1, rb, D_ATTN, MOBA_BLOCK), lambda b, i: (b, i, 0, 0)),
            pl.BlockSpec((1, rb, D_ATTN, MOBA_BLOCK), lambda b, i: (b, i, 0, 0)),
        ],
        out_shape=[
            jax.ShapeDtypeStruct((bsz, t, D_SSM), F32),
            jax.ShapeDtypeStruct((bsz, t, D_ATTN), BF16),
            jax.ShapeDtypeStruct((bsz, nb, 1, D_ATTN), F32),
            jax.ShapeDtypeStruct((bsz, nb, D_ATTN, MOBA_BLOCK), BF16),
            jax.ShapeDtypeStruct((bsz, nb, D_ATTN, MOBA_BLOCK), BF16),
        ],
        compiler_params=_cparams(2),
        name="inproj",
    )(x, g, w_u, w_kqvt, kg, qg)


def _s5prep_kernel(lre_ref, lim_ref, ls_ref, bre_ref, bim_ref, are_ref, aim_ref, bbre_ref, bbim_ref):
    a_re = jnp.minimum(lre_ref[...], -1e-4)
    a_im = lim_ref[...]
    step = jnp.exp(ls_ref[...])
    mag = jnp.exp(step * a_re)
    ab_re = mag * jnp.cos(step * a_im)
    ab_im = mag * jnp.sin(step * a_im)
    den = a_re * a_re + a_im * a_im
    p = ab_re - 1.0
    f_re = (p * a_re + ab_im * a_im) / den
    f_im = (ab_im * a_re - p * a_im) / den
    are_ref[...] = ab_re
    aim_ref[...] = ab_im
    bbre_ref[...] = f_re[:, None, :] * bre_ref[...] - f_im[:, None, :] * bim_ref[...]
    bbim_ref[...] = f_re[:, None, :] * bim_ref[...] + f_im[:, None, :] * bre_ref[...]


def _s5prep(lam_re, lam_im, log_step, b_re_t, b_im_t):
    g, n = lam_re.shape
    c = b_re_t.shape[1]
    return pl.pallas_call(
        _s5prep_kernel,
        out_shape=[jax.ShapeDtypeStruct((g, n), F32), jax.ShapeDtypeStruct((g, n), F32),
                   jax.ShapeDtypeStruct((g, c, n), F32), jax.ShapeDtypeStruct((g, c, n), F32)],
        name="s5prep",
    )(lam_re, lam_im, log_step, b_re_t, b_im_t)


S5_COLS = N_GROUPS * SSM_STATE
S5_NT = 256
S5_SCAN_COLS = 512
S5_PARTS = 4


def _s5_kernel(u_ref, bwre_ref, bwim_ref, are_ref, aim_ref, cwre_ref, cwim_ref, d_ref, wglu_ref,
               bglu_ref, gn_ref, o_ref, sre_ref, sim_ref, stre_ref, stim_ref, ut_ref, yt_ref):
    bsz, tt, _ = u_ref.shape
    n_slab = D_SSM // LANES

    @pl.when(pl.program_id(0) == 0)
    def _():
        stre_ref[...] = jnp.zeros_like(stre_ref)
        stim_ref[...] = jnp.zeros_like(stim_ref)

    for b in range(bsz):
        for s in range(n_slab):
            ut_ref[s, pl.ds(b, tt, stride=bsz), :] = u_ref[b, :, s * LANES:(s + 1) * LANES]
    n_parts = S5_PARTS
    tq = tt // n_parts
    n_chunks = S5_COLS // S5_SCAN_COLS
    chunks = [slice(cc * S5_SCAN_COLS, (cc + 1) * S5_SCAN_COLS) for cc in range(n_chunks)]
    a_re = [are_ref[:, cs] for cs in chunks]
    a_im = [aim_ref[:, cs] for cs in chunks]
    state = [(stre_ref[:, cs], stim_ref[:, cs]) for cs in chunks]
    half = S5_COLS // 2

    def part_rows(q):
        return slice(q * tq * bsz, (q + 1) * tq * bsz)

    def bu(q):
        rs = part_rows(q)
        slabs = [ut_ref[s, rs, :].astype(BF16) for s in range(n_slab)]
        for j in range(S5_COLS // S5_NT):
            cs = slice(j * S5_NT, (j + 1) * S5_NT)
            sre_ref[rs, cs] = jnp.dot(slabs[j // 2], bwre_ref[j], preferred_element_type=F32)
            sim_ref[rs, cs] = jnp.dot(slabs[j // 2], bwim_ref[j], preferred_element_type=F32)

    def scan(q):
        for cc, cs in enumerate(chunks):
            s_re, s_im = state[cc]
            for t in range(q * tq, (q + 1) * tq):
                rs = slice(t * bsz, (t + 1) * bsz)
                n_re = a_re[cc] * s_re - a_im[cc] * s_im + sre_ref[rs, cs]
                n_im = a_re[cc] * s_im + a_im[cc] * s_re + sim_ref[rs, cs]
                sre_ref[rs, cs] = n_re
                sim_ref[rs, cs] = n_im
                s_re, s_im = n_re, n_im
            state[cc] = (s_re, s_im)

    y_parts = {}

    def cs_out(q):
        rs = part_rows(q)
        ys = []
        for n in range(2):
            ks = slice(n * half, (n + 1) * half)
            ys.append(jnp.dot(sre_ref[rs, ks].astype(BF16), cwre_ref[n], preferred_element_type=F32)
                      - jnp.dot(sim_ref[rs, ks].astype(BF16), cwim_ref[n], preferred_element_type=F32))
        u = jnp.concatenate([ut_ref[s, rs, :] for s in range(n_slab)], axis=1)
        y_parts[q] = jax.nn.gelu(jnp.concatenate(ys, axis=1) + d_ref[...] * u)

    def glu(q):
        rs = part_rows(q)
        y = y_parts.pop(q)
        z = jnp.dot(y.astype(BF16), wglu_ref[...], preferred_element_type=F32) + bglu_ref[...]
        yn = _rms_rows(y * jax.nn.sigmoid(z), gn_ref[...])
        for s in range(n_slab):
            yt_ref[s, rs, :] = yn[:, s * LANES:(s + 1) * LANES]

    bu(0)
    for q in range(n_parts + 3):
        if q + 1 < n_parts:
            bu(q + 1)
        if q < n_parts:
            scan(q)
        if 0 <= q - 1 < n_parts:
            cs_out(q - 1)
        if 0 <= q - 2 < n_parts:
            glu(q - 2)
    for cc, cs in enumerate(chunks):
        stre_ref[:, cs], stim_ref[:, cs] = state[cc]

    for b in range(bsz):
        for s in range(n_slab):
            o_ref[b, :, s * LANES:(s + 1) * LANES] = yt_ref[s, pl.ds(b, tt, stride=bsz), :].astype(o_ref.dtype)


def _s5(u, bw_re, bw_im, a_re8, a_im8, cw_re, cw_im, d, w_glu, b_glu, gn, tt):
    bsz, t, _ = u.shape
    assert bsz == SUBLANES
    rows = tt * bsz
    full = lambda a: pl.BlockSpec(a.shape, lambda i: (0,) * a.ndim)
    consts = (bw_re, bw_im, a_re8, a_im8, cw_re, cw_im, d, w_glu, b_glu, gn)
    slab_rows = pltpu.VMEM((D_SSM // LANES, rows, LANES), F32)
    return pl.pallas_call(
        _s5_kernel,
        grid=(t // tt,),
        in_specs=[pl.BlockSpec((bsz, tt, D_SSM), lambda i: (0, i, 0))] + [full(a) for a in consts],
        out_specs=pl.BlockSpec((bsz, tt, D_SSM), lambda i: (0, i, 0)),
        out_shape=jax.ShapeDtypeStruct((bsz, t, D_SSM), BF16),
        scratch_shapes=[pltpu.VMEM((rows, S5_COLS), F32), pltpu.VMEM((rows, S5_COLS), F32),
                        pltpu.VMEM((bsz, S5_COLS), F32), pltpu.VMEM((bsz, S5_COLS), F32),
                        slab_rows, slab_rows],
        compiler_params=_cparams(1),
        name="s5",
    )(u, *consts)


def _t5_bucket_table(n):
    d = np.arange(n)
    max_exact = N_BUCKETS // 2
    nf = np.maximum(d, max_exact).astype(np.float32)
    large = max_exact + (np.log(nf / max_exact) / math.log(MAX_DISTANCE / max_exact)
                         * (N_BUCKETS - max_exact)).astype(np.int32)
    large = np.minimum(large, N_BUCKETS - 1)
    return np.where(d < max_exact, d, large).astype(np.int32)


def _bucket_tiles():
    tbl = _t5_bucket_table(2 * MOBA_BLOCK)
    r = np.arange(MOBA_BLOCK)[:, None]
    c = np.arange(MOBA_BLOCK)[None, :]
    d_own = c - r
    own = np.where(d_own >= 0, tbl[np.maximum(d_own, 0)], -1).astype(np.int32)
    prev = tbl[d_own + MOBA_BLOCK].astype(np.int32)
    return own, prev


def _biastile_kernel(rb_ref, own_ref, prev_ref, o_ref):
    h = pl.program_id(0)
    own = own_ref[...]
    prev = prev_ref[...]
    t_own = jnp.full(own.shape, NEG, F32)
    t_prev = jnp.zeros(prev.shape, F32)
    for b in range(N_BUCKETS):
        v = rb_ref[b, h] * LOG2E
        t_own = jnp.where(own == b, v, t_own)
        t_prev = jnp.where(prev == b, v, t_prev)
    o_ref[0, 0] = t_own
    o_ref[0, 1] = t_prev


def _biastile(rel_bias, own, prev):
    blk = own.shape
    return pl.pallas_call(
        _biastile_kernel,
        grid=(N_HEADS,),
        in_specs=[pl.BlockSpec(memory_space=pltpu.SMEM),
                  pl.BlockSpec(blk, lambda h: (0, 0)), pl.BlockSpec(blk, lambda h: (0, 0))],
        out_specs=pl.BlockSpec((1, 2) + blk, lambda h: (h, 0, 0, 0)),
        out_shape=jax.ShapeDtypeStruct((N_HEADS, 2) + blk, F32),
        compiler_params=_cparams(1),
        name="biastile",
    )(rel_bias, own, prev)


MOBA_BIAS_PARTS = 3
MOBA_ONES_ROWS = 16
MOBA_EXTRA_ROWS = 32


def _moba_selector(nb):
    e = np.zeros((nb, MOBA_BLOCK, LANES), np.float32)
    for j in range(nb):
        e[j, :, j] = 1.0
        e[j, :, nb:nb + MOBA_BIAS_PARTS] = 1.0
    return e


def _moba_kernel(rb_ref, qt_ref, k_ref, vt_ref, kmean_ref, bias_ref, esel_ref, o_ref,
                 qpad_ref, sel_ref, m_ref, acc_ref):
    i = pl.program_id(1)
    nb = kmean_ref.shape[1]
    tq = qt_ref.shape[3]
    pair_w = 2 * HEAD_DIM
    heads = range(N_HEADS)

    def pair_cols(h):
        return slice((h // 2) * pair_w, (h // 2 + 1) * pair_w)

    blk = lax.broadcasted_iota(jnp.int32, (nb, tq), 0)
    for h in heads:
        qt = qt_ref[0, 0, h * HEAD_DIM:(h + 1) * HEAD_DIM, :]
        zero = jnp.zeros_like(qt)
        qpad = jnp.concatenate([qt, zero] if h % 2 == 0 else [zero, qt], axis=0)
        gate = jnp.dot(kmean_ref[0, :, pair_cols(h)], qpad, preferred_element_type=F32)
        g = jnp.where(blk < i, gate, -jnp.inf)
        sel = jnp.zeros((nb, tq), F32)
        for _ in range(MOBA_TOPK):
            mx = jnp.max(g, axis=0, keepdims=True)
            cand = jnp.where((g == mx) & (mx > -jnp.inf), blk, nb)
            idx = jnp.min(cand, axis=0, keepdims=True)
            pick = blk == idx
            sel = jnp.where(pick, 1.0, sel)
            g = jnp.where(pick, -jnp.inf, g)
        mask = jnp.where(sel > 0.0, 0.0, NEG)
        sel_ref[h] = mask
        c = jnp.full((nb, tq), rb_ref[N_BUCKETS - 1, h] * LOG2E, F32)
        cb = jnp.zeros((nb, tq), F32)
        for r in range(MOBA_BIAS_PARTS):
            pc = c.astype(BF16).astype(F32)
            cb = jnp.where(blk == r, pc, cb)
            c = c - pc
        pad = jnp.zeros((pair_w - 2 * nb, tq), BF16)
        qpad_ref[h] = jnp.concatenate([qpad, mask.astype(BF16), cb.astype(BF16), pad], axis=0)

    def key_block(h, j):
        r0 = pl.multiple_of(j * MOBA_BLOCK, MOBA_BLOCK)
        return k_ref[0, pl.ds(r0, MOBA_BLOCK), pair_cols(h)]

    def near_scores(h, j, bias):
        return jnp.dot(key_block(h, j), qpad_ref[h, 0:pair_w], preferred_element_type=F32) + bias

    def far_scores(h, j):
        lhs = jnp.concatenate([key_block(h, j), esel_ref[j, :, 0:MOBA_EXTRA_ROWS]], axis=1)
        return jnp.dot(lhs, qpad_ref[h, 0:pair_w + MOBA_EXTRA_ROWS],
                       preferred_element_type=F32)

    ones = jnp.ones((MOBA_ONES_ROWS, MOBA_BLOCK), BF16)

    def tile_pass(tiles, state):
        queue = {}
        for t in range(min(MOBA_LOOKAHEAD, len(tiles))):
            queue[t] = tiles[t][2]()
        for t in range(len(tiles)):
            if t + MOBA_LOOKAHEAD < len(tiles):
                queue[t + MOBA_LOOKAHEAD] = tiles[t + MOBA_LOOKAHEAD][2]()
            h, j, _ = tiles[t]
            s = queue.pop(t)
            m_new = jnp.max(s, axis=0, keepdims=True)
            if h in state:
                m_new = jnp.maximum(state[h][0], m_new)
            p = jnp.exp2(s - m_new).astype(BF16)
            v1 = jnp.concatenate([vt_ref[0, j, h * HEAD_DIM:(h + 1) * HEAD_DIM, :], ones], axis=0)
            acc_new = jnp.dot(v1, p, preferred_element_type=F32)
            if h in state:
                acc_new = jnp.exp2(state[h][0] - m_new) * state[h][1] + acc_new
            state[h] = (m_new, acc_new)
        return state

    def load_state():
        return {h: (m_ref[h], acc_ref[h]) for h in heads}

    def store_state(state):
        for h in heads:
            m_ref[h], acc_ref[h] = state[h]

    def own_tiles():
        return [(h, i, functools.partial(near_scores, h, i, bias_ref[h, 0])) for h in heads]

    @pl.when(i == 0)
    def _():
        store_state(tile_pass(own_tiles(), {}))

    def far_tiles(j0, count):
        return [(h, j0 + jj, functools.partial(far_scores, h, j0 + jj)) for jj in range(count) for h in heads]

    n_far = jnp.maximum(i - 1, 0)
    rem = n_far % MOBA_FAR_GROUP
    for r in range(MOBA_FAR_GROUP):
        @pl.when((i >= 1) & (rem == r))
        def _(r=r):
            j = i - 1
            prev = [(h, j, functools.partial(near_scores, h, j, bias_ref[h, 1] + sel_ref[h, pl.ds(j, 1), :]))
                    for h in heads]
            store_state(tile_pass(own_tiles() + prev + far_tiles(0, r), {}))

    def far_group(it, flat):
        state = tile_pass(far_tiles(rem + MOBA_FAR_GROUP * it, MOBA_FAR_GROUP), {h: flat[h] for h in heads})
        return tuple(state[h] for h in heads)

    state0 = load_state()
    final = lax.fori_loop(0, n_far // MOBA_FAR_GROUP, far_group, tuple(state0[h] for h in heads))

    for h in heads:
        acc = final[h][1]
        o_ref[0, 0, h * HEAD_DIM:(h + 1) * HEAD_DIM, :] = (
            acc[:HEAD_DIM] / acc[HEAD_DIM:HEAD_DIM + 1]).astype(o_ref.dtype)


def _moba(rel_bias, qt, k, vt, kmean, bias_tiles):
    bsz, nb, _, tq = qt.shape
    t = k.shape[1]
    assert nb + MOBA_BIAS_PARTS <= MOBA_EXTRA_ROWS
    esel = jnp.asarray(_moba_selector(nb), BF16)
    once = lambda a: pl.BlockSpec(a.shape, lambda b, i: (0,) * a.ndim, pipeline_mode=pl.Buffered(1))
    return pl.pallas_call(
        _moba_kernel,
        grid=(bsz, nb),
        in_specs=[
            pl.BlockSpec(memory_space=pltpu.SMEM),
            pl.BlockSpec((1, 1, D_ATTN, tq), lambda b, i: (b, i, 0, 0)),
            pl.BlockSpec((1, t, D_ATTN), lambda b, i: (b, 0, 0)),
            pl.BlockSpec((1, nb, D_ATTN, tq), lambda b, i: (b, 0, 0, 0)),
            pl.BlockSpec((1, nb, D_ATTN), lambda b, i: (b, 0, 0)),
            once(bias_tiles), once(esel),
        ],
        out_specs=pl.BlockSpec((1, 1, D_ATTN, tq), lambda b, i: (b, i, 0, 0)),
        out_shape=jax.ShapeDtypeStruct((bsz, nb, D_ATTN, tq), BF16),
        scratch_shapes=[pltpu.VMEM((N_HEADS, 4 * HEAD_DIM, tq), BF16), pltpu.VMEM((N_HEADS, nb, tq), F32),
                        pltpu.VMEM((N_HEADS, 1, tq), F32),
                        pltpu.VMEM((N_HEADS, HEAD_DIM + MOBA_ONES_ROWS, tq), F32)],
        compiler_params=_cparams(2),
        name="moba",
    )(rel_bias, qt, k, vt, kmean, bias_tiles, esel)


def _memkv_kernel(mem_ref, g_ref, wkv_ref, kg_ref, k_ref, v_ref):
    mn = _rms_rows(mem_ref[0], g_ref[...]).astype(BF16)
    kv = jnp.dot(mn, wkv_ref[...], preferred_element_type=F32)
    ks = []
    for hh in range(X_HEADS):
        cs = slice(hh * X_HEAD_DIM, (hh + 1) * X_HEAD_DIM)
        ks.append(_rms_rows(kv[:, cs], kg_ref[...]))
    k_ref[0] = jnp.concatenate(ks, axis=1).astype(BF16)
    v_ref[0] = kv[:, D_X:].astype(BF16)


def _memkv(mem, g, w_kv, kg):
    bsz, m, d = mem.shape
    full = lambda a: pl.BlockSpec(a.shape, lambda b: (0,) * a.ndim)
    return pl.pallas_call(
        _memkv_kernel,
        grid=(bsz,),
        in_specs=[pl.BlockSpec((1, m, d), lambda b: (b, 0, 0)), full(g), full(w_kv), full(kg)],
        out_specs=[pl.BlockSpec((1, m, D_X), lambda b: (b, 0, 0))] * 2,
        out_shape=[jax.ShapeDtypeStruct((bsz, m, D_X), BF16)] * 2,
        compiler_params=_cparams(1),
        name="memkv",
    )(mem, g, w_kv, kg)


OUTX_PIECE = 512


def _outx_kernel(x_ref, ys_ref, yat_ref, ga_ref, wout_ref, gx_ref, wq_ref, qg_ref, kx_ref, vx_ref,
                 wo_ref, o_ref):
    per_piece = OUTX_PIECE // MOBA_BLOCK
    pieces = range(yat_ref.shape[1] // per_piece)
    rows = [slice(r * OUTX_PIECE, (r + 1) * OUTX_PIECE) for r in pieces]
    ya_n = []
    for r in pieces:
        parts = []
        for blk in range(r * per_piece, (r + 1) * per_piece):
            ya = yat_ref[0, blk].astype(F32)
            ms = jnp.mean(ya * ya, axis=0, keepdims=True)
            yan = ya * lax.rsqrt(ms + EPS) * jnp.tile(ga_ref[...], (1, ya.shape[1] // LANES))
            parts.append(yan.T)
        ya_n.append(jnp.concatenate(parts, axis=0).astype(BF16))
    h1 = [x_ref[0, rows[r]]
          + jnp.dot(ys_ref[0, rows[r]], wout_ref[:D_SSM], preferred_element_type=F32)
          + jnp.dot(ya_n[r], wout_ref[D_SSM:], preferred_element_type=F32) for r in pieces]
    hn = [_rms_rows(h1[r], gx_ref[...]).astype(BF16) for r in pieces]
    q = [jnp.dot(hn[r], wq_ref[...], preferred_element_type=F32) for r in pieces]
    outs = [[] for _ in pieces]
    head_cols = [slice(hh * X_HEAD_DIM, (hh + 1) * X_HEAD_DIM) for hh in range(X_HEADS)]

    def scores(hh):
        cs = head_cols[hh]
        qh = [_rms_rows(q[r][:, cs], qg_ref[...]).astype(BF16) for r in pieces]
        return [lax.dot_general(qh[r], kx_ref[0, :, cs], (((1,), (1,)), ((), ())),
                                preferred_element_type=F32) * (X_HEAD_DIM ** -0.5) for r in pieces]

    s_next = scores(0)
    for hh in range(X_HEADS):
        s = s_next
        if hh + 1 < X_HEADS:
            s_next = scores(hh + 1)
        for r in pieces:
            e = jnp.exp(s[r] - jnp.max(s[r], axis=-1, keepdims=True))
            prob = e / jnp.sum(e, axis=-1, keepdims=True)
            outs[r].append(jnp.dot(prob.astype(BF16), vx_ref[0, :, head_cols[hh]],
                                   preferred_element_type=F32))
    for r in pieces:
        o = jnp.concatenate(outs[r], axis=1).astype(BF16)
        o_ref[0, rows[r]] = h1[r] + jnp.dot(o, wo_ref[...], preferred_element_type=F32)


def _outx(x, ys, yat, ga, w_out, gx, w_q, qg, kx, vx, w_o, tm):
    bsz, t, d = x.shape
    m = kx.shape[1]
    rb = tm // MOBA_BLOCK
    full = lambda a: pl.BlockSpec(a.shape, lambda b, i: (0,) * a.ndim)
    return pl.pallas_call(
        _outx_kernel,
        grid=(bsz, t // tm),
        in_specs=[
            pl.BlockSpec((1, tm, d), lambda b, i: (b, i, 0)),
            pl.BlockSpec((1, tm, D_SSM), lambda b, i: (b, i, 0)),
            pl.BlockSpec((1, rb, D_ATTN, MOBA_BLOCK), lambda b, i: (b, i, 0, 0)),
            full(ga), full(w_out), full(gx), full(w_q), full(qg),
            pl.BlockSpec((1, m, D_X), lambda b, i: (b, 0, 0)),
            pl.BlockSpec((1, m, D_X), lambda b, i: (b, 0, 0)),
            full(w_o),
        ],
        out_specs=pl.BlockSpec((1, tm, d), lambda b, i: (b, i, 0)),
        out_shape=jax.ShapeDtypeStruct((bsz, t, d), F32),
        compiler_params=_cparams(2),
        name="outx",
    )(x, ys, yat, ga, w_out, gx, w_q, qg, kx, vx, w_o)


FFN_CHUNK = 1408


def _ffn_kernel(h_ref, g_ref, wup_ref, cw_ref, cb_ref, wdn_ref, o_ref, tail_ref):
    tm = h_ref.shape[1]

    @pl.when(pl.program_id(1) == 0)
    def _():
        tail_ref[...] = jnp.zeros_like(tail_ref)

    h = h_ref[0]
    hn = _rms_rows(h, g_ref[...]).astype(BF16)
    acc = h

    def conv(up, cols):
        ext = jnp.concatenate([tail_ref[:, cols], up], axis=0)
        tail_ref[:, cols] = up[tm - SUBLANES:]
        w = cw_ref[:, cols]
        return (w[0:1] * ext[SUBLANES - 2:SUBLANES - 2 + tm]
                + w[1:2] * ext[SUBLANES - 1:SUBLANES - 1 + tm]
                + w[2:3] * up + cb_ref[:, cols])

    n_chunks = D_FF // FFN_CHUNK
    cols = [(slice(c * FFN_CHUNK, (c + 1) * FFN_CHUNK),
             slice(D_FF + c * FFN_CHUNK, D_FF + (c + 1) * FFN_CHUNK)) for c in range(n_chunks)]
    ups = [tuple(jnp.dot(hn, wup_ref[:, cs], preferred_element_type=F32) for cs in cols[c])
           for c in range(n_chunks)]
    for c in range(n_chunks):
        act = (jax.nn.silu(conv(ups[c][0], cols[c][0])) * conv(ups[c][1], cols[c][1])).astype(BF16)
        acc = acc + jnp.dot(act, wdn_ref[cols[c][0]], preferred_element_type=F32)
    o_ref[0] = acc


def _ffn(h, g, w_up, conv_w, conv_b, w_down, tm):
    bsz, t, d = h.shape
    full = lambda a: pl.BlockSpec(a.shape, lambda b, i: (0,) * a.ndim, pipeline_mode=pl.Buffered(1))
    return pl.pallas_call(
        _ffn_kernel,
        grid=(bsz, t // tm),
        in_specs=[pl.BlockSpec((1, tm, d), lambda b, i: (b, i, 0)),
                  full(g), full(w_up), full(conv_w), full(conv_b), full(w_down)],
        out_specs=pl.BlockSpec((1, tm, d), lambda b, i: (b, i, 0)),
        out_shape=jax.ShapeDtypeStruct((bsz, t, d), F32),
        scratch_shapes=[pltpu.VMEM((SUBLANES, 2 * D_FF), F32)],
        compiler_params=_cparams(2),
        name="ffn",
    )(h, g, w_up, conv_w, conv_b, w_down)


def _block_diag_b(bb):
    g, c, n = bb.shape
    per_tile = S5_NT // n
    per_slab = LANES // c
    tiles = S5_COLS // S5_NT
    place = np.zeros((tiles, per_slab, per_tile), np.float32)
    for j in range(tiles):
        for q in range(per_tile):
            place[j, (j * per_tile + q) % per_slab, q] = 1.0
    out = jnp.einsum('jqcn,jgq->jgcqn', bb.reshape(tiles, per_tile, c, n), jnp.asarray(place))
    return out.reshape(tiles, LANES, S5_NT).astype(BF16)


def _block_diag_c(cm):
    g, c, n = cm.shape
    per_tile = g // 2
    out = jnp.einsum('tqcn,qp->tqnpc', cm.reshape(2, per_tile, c, n), jnp.eye(per_tile, dtype=F32))
    return out.reshape(2, per_tile * n, per_tile * c).astype(BF16)


def kernel(x, mem, norm_mix, w_in, attn_qn, attn_kn, ssm_lam_re, ssm_lam_im, ssm_log_step, ssm_b_re, ssm_b_im, ssm_c_re, ssm_c_im, ssm_d, ssm_w_glu, ssm_b_glu, gnorm_ssm, gnorm_attn, w_out, rel_bias, norm_xattn, norm_mem, x_wq, x_wkv, x_wo, x_qn, x_kn, norm_ffn, ffn_w_up, ffn_conv_w, ffn_conv_b, ffn_w_down):
    bsz, t, d = x.shape
    assert bsz == SUBLANES and t % (2 * max(INPROJ_PIECE, OUTX_PIECE)) == 0
    assert norm_mix.shape[0] == 1
    l = 0
    row = lambda v: v.reshape(1, -1).astype(F32)

    w = w_in[l]
    w_u = w[:, :D_SSM].astype(BF16)
    w_kqvt = jnp.concatenate([w[:, D_SSM + D_ATTN:D_SSM + 2 * D_ATTN], w[:, D_SSM:D_SSM + D_ATTN],
                              w[:, D_SSM + 2 * D_ATTN:]], axis=1).T.astype(BF16)
    lane_rep = lambda v: jnp.broadcast_to(v[:, None], (v.shape[0], LANES)).astype(F32)
    qg = lane_rep(jnp.tile(attn_qn[l], N_HEADS) * (HEAD_DIM ** -0.5))
    kg = lane_rep(jnp.tile(attn_kn[l], N_HEADS))
    u, k, kmean, qt, vt = _inproj(x, row(norm_mix[l]), w_u, w_kqvt, kg, qg, tm=2 * INPROJ_PIECE)

    a_re, a_im, bb_re, bb_im = _s5prep(ssm_lam_re[l], ssm_lam_im[l], ssm_log_step[l][:, None],
                                       jnp.swapaxes(ssm_b_re[l], 1, 2), jnp.swapaxes(ssm_b_im[l], 1, 2))
    a_re8 = jnp.broadcast_to(a_re.reshape(1, -1), (SUBLANES, S5_COLS))
    a_im8 = jnp.broadcast_to(a_im.reshape(1, -1), (SUBLANES, S5_COLS))
    ys = _s5(u, _block_diag_b(bb_re), _block_diag_b(bb_im), a_re8, a_im8,
             _block_diag_c(ssm_c_re[l]), _block_diag_c(ssm_c_im[l]), row(ssm_d[l]),
             ssm_w_glu[l].astype(BF16), row(ssm_b_glu[l]), row(gnorm_ssm[l]), tt=S5_TIME_TILE)

    own, prev = _bucket_tiles()
    bias_tiles = _biastile(rel_bias, jnp.asarray(own), jnp.asarray(prev))
    nb = t // MOBA_BLOCK
    yat = _moba(rel_bias, qt, k, vt, kmean.reshape(bsz, nb, D_ATTN).astype(BF16), bias_tiles)

    kx, vx = _memkv(mem, row(norm_mem[l]), x_wkv[l].astype(BF16), row(x_kn[l]))
    ga = jnp.broadcast_to(gnorm_attn[l][:, None], (D_ATTN, LANES)).astype(F32)
    h2 = _outx(x, ys, yat, ga, w_out[l].astype(BF16), row(norm_xattn[l]), x_wq[l].astype(BF16),
               row(x_qn[l]), kx, vx, x_wo[l].astype(BF16), tm=2 * OUTX_PIECE)

    return _ffn(h2, row(norm_ffn[l]), ffn_w_up[l].astype(BF16), ffn_conv_w[l].reshape(3, 2 * D_FF),
                row(ffn_conv_b[l]), ffn_w_down[l].astype(BF16), tm=FFN_ROWS)
```

```python
import functools
import math

import numpy as np
import jax
import jax.numpy as jnp
from jax import lax
from jax.experimental import pallas as pl
from jax.experimental.pallas import tpu as pltpu

F32 = jnp.float32
BF16 = jnp.bfloat16

EPS = 1e-6
NEG = -1e30
LOG2E = 1.4426950408889634

D_SSM = 512
N_GROUPS = 32
SSM_STATE = 64
D_ATTN = 512
HEAD_DIM = 64
N_HEADS = 8
MOBA_BLOCK = 256
MOBA_TOPK = 3
MOBA_LOOKAHEAD = 6
MOBA_FAR_GROUP = 8
N_BUCKETS = 32
MAX_DISTANCE = 128
X_HEADS = 4
X_HEAD_DIM = 128
D_X = 512
D_FF = 2816
LANES = 128
SUBLANES = 8
V7X_VMEM_BYTES = 64 * 1024 * 1024
VMEM_LIMIT = V7X_VMEM_BYTES * 7 // 8

S5_TIME_TILE = 128
FFN_ROWS = 512


def _cparams(n_axes):
    return pltpu.CompilerParams(
        dimension_semantics=("arbitrary",) * n_axes, vmem_limit_bytes=VMEM_LIMIT)


def _rms_rows(x, g):
    ms = jnp.mean(x * x, axis=-1, keepdims=True)
    return x * lax.rsqrt(ms + EPS) * g


INPROJ_PIECE = 512


def _inproj_kernel(x_ref, g_ref, wu_ref, wkqvt_ref, kg_ref, qg_ref,
                   u_ref, k_ref, kmean_ref, qt_ref, vt_ref):
    tp = INPROJ_PIECE
    pieces = range(x_ref.shape[1] // tp)
    per_piece = tp // MOBA_BLOCK
    rows = [slice(r * tp, (r + 1) * tp) for r in pieces]
    hn = [_rms_rows(x_ref[0, rows[r]], g_ref[...]).astype(BF16) for r in pieces]
    p_u, p_t = [], []
    for r in pieces:
        p_u.append(jnp.dot(hn[r], wu_ref[...], preferred_element_type=F32))
        p_t.append(lax.dot_general(wkqvt_ref[...], hn[r], (((1,), (1,)), ((), ())),
                                   preferred_element_type=F32))

    def head_norm_t(xt, gain_ref):
        x3 = xt.reshape(N_HEADS, HEAD_DIM, tp)
        ms = jnp.mean(x3 * x3, axis=1, keepdims=True)
        return (x3 * lax.rsqrt(ms + EPS)).reshape(D_ATTN, tp) * jnp.tile(gain_ref[...], (1, tp // LANES))

    for r in pieces:
        u_ref[0, rows[r]] = p_u[r]
        kn = head_norm_t(p_t[r][:D_ATTN], kg_ref).T
        k_ref[0, rows[r]] = (kn * LOG2E).astype(BF16)
        qn = head_norm_t(p_t[r][D_ATTN:2 * D_ATTN], qg_ref)
        vt = p_t[r][2 * D_ATTN:]
        for b in range(per_piece):
            cs = slice(b * MOBA_BLOCK, (b + 1) * MOBA_BLOCK)
            qt_ref[0, r * per_piece + b] = qn[:, cs].astype(BF16)
            vt_ref[0, r * per_piece + b] = vt[:, cs].astype(BF16)
            kmean_ref[0, r * per_piece + b] = jnp.mean(kn[cs], axis=0, keepdims=True)


def _inproj(x, g, w_u, w_kqvt, kg, qg, tm):
    bsz, t, d = x.shape
    nb = t // MOBA_BLOCK
    rb = tm // MOBA_BLOCK
    full = lambda shp: pl.BlockSpec(shp, lambda b, i: (0,) * len(shp))
    return pl.pallas_call(
        _inproj_kernel,
        grid=(bsz, t // tm),
        in_specs=[
            pl.BlockSpec((1, tm, d), lambda b, i: (b, i, 0)),
            full((1, d)), full(w_u.shape), full(w_kqvt.shape), full(kg.shape), full(qg.shape),
        ],
        out_specs=[
            pl.BlockSpec((1, tm, D_SSM), lambda b, i: (b, i, 0)),
            pl.BlockSpec((1, tm, D_ATTN), lambda b, i: (b, i, 0)),
            pl.BlockSpec((1, rb, 1, D_ATTN), lambda b, i: (b, i, 0, 0)),
            pl.BlockSpec((1, rb, D_ATTN, MOBA_BLOCK), lambda b, i: (b, i, 0, 0)),
            pl.BlockSpec((1, rb, D_ATTN, MOBA_BLOCK), lambda b, i: (b, i, 0, 0)),
        ],
        out_shape=[
            jax.ShapeDtypeStruct((bsz, t, D_SSM), F32),
            jax.ShapeDtypeStruct((bsz, t, D_ATTN), BF16),
            jax.ShapeDtypeStruct((bsz, nb, 1, D_ATTN), F32),
            jax.ShapeDtypeStruct((bsz, nb, D_ATTN, MOBA_BLOCK), BF16),
            jax.ShapeDtypeStruct((bsz, nb, D_ATTN, MOBA_BLOCK), BF16),
        ],
        compiler_params=_cparams(2),
        name="inproj",
    )(x, g, w_u, w_kqvt, kg, qg)


def _s5prep_kernel(lre_ref, lim_ref, ls_ref, bre_ref, bim_ref, are_ref, aim_ref, bbre_ref, bbim_ref):
    a_re = jnp.minimum(lre_ref[...], -1e-4)
    a_im = lim_ref[...]
    step = jnp.exp(ls_ref[...])
    mag = jnp.exp(step * a_re)
    ab_re = mag * jnp.cos(step * a_im)
    ab_im = mag * jnp.sin(step * a_im)
    den = a_re * a_re + a_im * a_im
    p = ab_re - 1.0
    f_re = (p * a_re + ab_im * a_im) / den
    f_im = (ab_im * a_re - p * a_im) / den
    are_ref[...] = ab_re
    aim_ref[...] = ab_im
    bbre_ref[...] = f_re[:, None, :] * bre_ref[...] - f_im[:, None, :] * bim_ref[...]
    bbim_ref[...] = f_re[:, None, :] * bim_ref[...] + f_im[:, None, :] * bre_ref[...]


def _s5prep(lam_re, lam_im, log_step, b_re_t, b_im_t):
    g, n = lam_re.shape
    c = b_re_t.shape[1]
    return pl.pallas_call(
        _s5prep_kernel,
        out_shape=[jax.ShapeDtypeStruct((g, n), F32), jax.ShapeDtypeStruct((g, n), F32),
                   jax.ShapeDtypeStruct((g, c, n), F32), jax.ShapeDtypeStruct((g, c, n), F32)],
        name="s5prep",
    )(lam_re, lam_im, log_step, b_re_t, b_im_t)


S5_COLS = N_GROUPS * SSM_STATE
S5_NT = 256
S5_SCAN_COLS = 512
S5_PARTS = 4


def _s5_kernel(u_ref, bwre_ref, bwim_ref, are_ref, aim_ref, cwre_ref, cwim_ref, d_ref, wglu_ref,
               bglu_ref, gn_ref, o_ref, sre_ref, sim_ref, stre_ref, stim_ref, ut_ref, yt_ref):
    bsz, tt, _ = u_ref.shape
    n_slab = D_SSM // LANES

    @pl.when(pl.program_id(0) == 0)
    def _():
        stre_ref[...] = jnp.zeros_like(stre_ref)
        stim_ref[...] = jnp.zeros_like(stim_ref)

    for b in range(bsz):
        for s in range(n_slab):
            ut_ref[s, pl.ds(b, tt, stride=bsz), :] = u_ref[b, :, s * LANES:(s + 1) * LANES]
    n_parts = S5_PARTS
    tq = tt // n_parts
    n_chunks = S5_COLS // S5_SCAN_COLS
    chunks = [slice(cc * S5_SCAN_COLS, (cc + 1) * S5_SCAN_COLS) for cc in range(n_chunks)]
    a_re = [are_ref[:, cs] for cs in chunks]
    a_im = [aim_ref[:, cs] for cs in chunks]
    state = [(stre_ref[:, cs], stim_ref[:, cs]) for cs in chunks]
    half = S5_COLS // 2

    def part_rows(q):
        return slice(q * tq * bsz, (q + 1) * tq * bsz)

    def bu(q):
        rs = part_rows(q)
        slabs = [ut_ref[s, rs, :].astype(BF16) for s in range(n_slab)]
        for j in range(S5_COLS // S5_NT):
            cs = slice(j * S5_NT, (j + 1) * S5_NT)
            sre_ref[rs, cs] = jnp.dot(slabs[j // 2], bwre_ref[j], preferred_element_type=F32)
            sim_ref[rs, cs] = jnp.dot(slabs[j // 2], bwim_ref[j], preferred_element_type=F32)

    def scan(q):
        for cc, cs in enumerate(chunks):
            s_re, s_im = state[cc]
            for t in range(q * tq, (q + 1) * tq):
                rs = slice(t * bsz, (t + 1) * bsz)
                n_re = a_re[cc] * s_re - a_im[cc] * s_im + sre_ref[rs, cs]
                n_im = a_re[cc] * s_im + a_im[cc] * s_re + sim_ref[rs, cs]
                sre_ref[rs, cs] = n_re
                sim_ref[rs, cs] = n_im
                s_re, s_im = n_re, n_im
            state[cc] = (s_re, s_im)

    y_parts = {}

    def cs_out(q):
        rs = part_rows(q)
        ys = []
        for n in range(2):
            ks = slice(n * half, (n + 1) * half)
            ys.append(jnp.dot(sre_ref[rs, ks].astype(BF16), cwre_ref[n], preferred_element_type=F32)
                      - jnp.dot(sim_ref[rs, ks].astype(BF16), cwim_ref[n], preferred_element_type=F32))
        u = jnp.concatenate([ut_ref[s, rs, :] for s in range(n_slab)], axis=1)
        y_parts[q] = jax.nn.gelu(jnp.concatenate(ys, axis=1) + d_ref[...] * u)

    def glu(q):
        rs = part_rows(q)
        y = y_parts.pop(q)
        z = jnp.dot(y.astype(BF16), wglu_ref[...], preferred_element_type=F32) + bglu_ref[...]
        yn = _rms_rows(y * jax.nn.sigmoid(z), gn_ref[...])
        for s in range(n_slab):
            yt_ref[s, rs, :] = yn[:, s * LANES:(s + 1) * LANES]

    bu(0)
    for q in range(n_parts + 3):
        if q + 1 < n_parts:
            bu(q + 1)
        if q < n_parts:
            scan(q)
        if 0 <= q - 1 < n_parts:
            cs_out(q - 1)
        if 0 <= q - 2 < n_parts:
            glu(q - 2)
    for cc, cs in enumerate(chunks):
        stre_ref[:, cs], stim_ref[:, cs] = state[cc]

    for b in range(bsz):
        for s in range(n_slab):
            o_ref[b, :, s * LANES:(s + 1) * LANES] = yt_ref[s, pl.ds(b, tt, stride=bsz), :].astype(o_ref.dtype)


def _s5(u, bw_re, bw_im, a_re8, a_im8, cw_re, cw_im, d, w_glu, b_glu, gn, tt):
    bsz, t, _ = u.shape
    assert bsz == SUBLANES
    rows = tt * bsz
    full = lambda a: pl.BlockSpec(a.shape, lambda i: (0,) * a.ndim)
    consts = (bw_re, bw_im, a_re8, a_im8, cw_re, cw_im, d, w_glu, b_glu, gn)
    slab_rows = pltpu.VMEM((D_SSM // LANES, rows, LANES), F32)
    return pl.pallas_call(
        _s5_kernel,
        grid=(t // tt,),
        in_specs=[pl.BlockSpec((bsz, tt, D_SSM), lambda i: (0, i, 0))] + [full(a) for a in consts],
        out_specs=pl.BlockSpec((bsz, tt, D_SSM), lambda i: (0, i, 0)),
        out_shape=jax.ShapeDtypeStruct((bsz, t, D_SSM), BF16),
        scratch_shapes=[pltpu.VMEM((rows, S5_COLS), F32), pltpu.VMEM((rows, S5_COLS), F32),
                        pltpu.VMEM((bsz, S5_COLS), F32), pltpu.VMEM((bsz, S5_COLS), F32),
                        slab_rows, slab_rows],
        compiler_params=_cparams(1),
        name="s5",
    )(u, *consts)


def _t5_bucket_table(n):
    d = np.arange(n)
    max_exact = N_BUCKETS // 2
    nf = np.maximum(d, max_exact).astype(np.float32)
    large = max_exact + (np.log(nf / max_exact) / math.log(MAX_DISTANCE / max_exact)
                         * (N_BUCKETS - max_exact)).astype(np.int32)
    large = np.minimum(large, N_BUCKETS - 1)
    return np.where(d < max_exact, d, large).astype(np.int32)


def _bucket_tiles():
    tbl = _t5_bucket_table(2 * MOBA_BLOCK)
    r = np.arange(MOBA_BLOCK)[:, None]
    c = np.arange(MOBA_BLOCK)[None, :]
    d_own = c - r
    own = np.where(d_own >= 0, tbl[np.maximum(d_own, 0)], -1).astype(np.int32)
    prev = tbl[d_own + MOBA_BLOCK].astype(np.int32)
    return own, prev


def _biastile_kernel(rb_ref, own_ref, prev_ref, o_ref):
    h = pl.program_id(0)
    own = own_ref[...]
    prev = prev_ref[...]
    t_own = jnp.full(own.shape, NEG, F32)
    t_prev = jnp.zeros(prev.shape, F32)
    for b in range(N_BUCKETS):
        v = rb_ref[b, h] * LOG2E
        t_own = jnp.where(own == b, v, t_own)
        t_prev = jnp.where(prev == b, v, t_prev)
    o_ref[0, 0] = t_own
    o_ref[0, 1] = t_prev


def _biastile(rel_bias, own, prev):
    blk = own.shape
    return pl.pallas_call(
        _biastile_kernel,
        grid=(N_HEADS,),
        in_specs=[pl.BlockSpec(memory_space=pltpu.SMEM),
                  pl.BlockSpec(blk, lambda h: (0, 0)), pl.BlockSpec(blk, lambda h: (0, 0))],
        out_specs=pl.BlockSpec((1, 2) + blk, lambda h: (h, 0, 0, 0)),
        out_shape=jax.ShapeDtypeStruct((N_HEADS, 2) + blk, F32),
        compiler_params=_cparams(1),
        name="biastile",
    )(rel_bias, own, prev)


MOBA_BIAS_PARTS = 3
MOBA_ONES_ROWS = 16
MOBA_EXTRA_ROWS = 32


def _moba_selector(nb):
    e = np.zeros((nb, MOBA_BLOCK, LANES), np.float32)
    for j in range(nb):
        e[j, :, j] = 1.0
        e[j, :, nb:nb + MOBA_BIAS_PARTS] = 1.0
    return e


def _moba_kernel(rb_ref, qt_ref, k_ref, vt_ref, kmean_ref, bias_ref, esel_ref, o_ref,
                 qpad_ref, sel_ref, m_ref, acc_ref):
    i = pl.program_id(1)
    nb = kmean_ref.shape[1]
    tq = qt_ref.shape[3]
    pair_w = 2 * HEAD_DIM
    heads = range(N_HEADS)

    def pair_cols(h):
        return slice((h // 2) * pair_w, (h // 2 + 1) * pair_w)

    blk = lax.broadcasted_iota(jnp.int32, (nb, tq), 0)
    for h in heads:
        qt = qt_ref[0, 0, h * HEAD_DIM:(h + 1) * HEAD_DIM, :]
        zero = jnp.zeros_like(qt)
        qpad = jnp.concatenate([qt, zero] if h % 2 == 0 else [zero, qt], axis=0)
        gate = jnp.dot(kmean_ref[0, :, pair_cols(h)], qpad, preferred_element_type=F32)
        g = jnp.where(blk < i, gate, -jnp.inf)
        sel = jnp.zeros((nb, tq), F32)
        for _ in range(MOBA_TOPK):
            mx = jnp.max(g, axis=0, keepdims=True)
            cand = jnp.where((g == mx) & (mx > -jnp.inf), blk, nb)
            idx = jnp.min(cand, axis=0, keepdims=True)
            pick = blk == idx
            sel = jnp.where(pick, 1.0, sel)
            g = jnp.where(pick, -jnp.inf, g)
        mask = jnp.where(sel > 0.0, 0.0, NEG)
        sel_ref[h] = mask
        c = jnp.full((nb, tq), rb_ref[N_BUCKETS - 1, h] * LOG2E, F32)
        cb = jnp.zeros((nb, tq), F32)
        for r in range(MOBA_BIAS_PARTS):
            pc = c.astype(BF16).astype(F32)
            cb = jnp.where(blk == r, pc, cb)
            c = c - pc
        pad = jnp.zeros((pair_w - 2 * nb, tq), BF16)
        qpad_ref[h] = jnp.concatenate([qpad, mask.astype(BF16), cb.astype(BF16), pad], axis=0)

    def key_block(h, j):
        r0 = pl.multiple_of(j * MOBA_BLOCK, MOBA_BLOCK)
        return k_ref[0, pl.ds(r0, MOBA_BLOCK), pair_cols(h)]

    def near_scores(h, j, bias):
        return jnp.dot(key_block(h, j), qpad_ref[h, 0:pair_w], preferred_element_type=F32) + bias

    def far_scores(h, j):
        lhs = jnp.concatenate([key_block(h, j), esel_ref[j, :, 0:MOBA_EXTRA_ROWS]], axis=1)
        return jnp.dot(lhs, qpad_ref[h, 0:pair_w + MOBA_EXTRA_ROWS],
                       preferred_element_type=F32)

    ones = jnp.ones((MOBA_ONES_ROWS, MOBA_BLOCK), BF16)

    def tile_pass(tiles, state):
        queue = {}
        for t in range(min(MOBA_LOOKAHEAD, len(tiles))):
            queue[t] = tiles[t][2]()
        for t in range(len(tiles)):
            if t + MOBA_LOOKAHEAD < len(tiles):
                queue[t + MOBA_LOOKAHEAD] = tiles[t + MOBA_LOOKAHEAD][2]()
            h, j, _ = tiles[t]
            s = queue.pop(t)
            m_new = jnp.max(s, axis=0, keepdims=True)
            if h in state:
                m_new = jnp.maximum(state[h][0], m_new)
            p = jnp.exp2(s - m_new).astype(BF16)
            v1 = jnp.concatenate([vt_ref[0, j, h * HEAD_DIM:(h + 1) * HEAD_DIM, :], ones], axis=0)
            acc_new = jnp.dot(v1, p, preferred_element_type=F32)
            if h in state:
                acc_new = jnp.exp2(state[h][0] - m_new) * state[h][1] + acc_new
            state[h] = (m_new, acc_new)
        return state

    def load_state():
        return {h: (m_ref[h], acc_ref[h]) for h in heads}

    def store_state(state):
        for h in heads:
            m_ref[h], acc_ref[h] = state[h]

    def own_tiles():
        return [(h, i, functools.partial(near_scores, h, i, bias_ref[h, 0])) for h in heads]

    @pl.when(i == 0)
    def _():
        store_state(tile_pass(own_tiles(), {}))

    def far_tiles(j0, count):
        return [(h, j0 + jj, functools.partial(far_scores, h, j0 + jj)) for jj in range(count) for h in heads]

    n_far = jnp.maximum(i - 1, 0)
    rem = n_far % MOBA_FAR_GROUP
    for r in range(MOBA_FAR_GROUP):
        @pl.when((i >= 1) & (rem == r))
        def _(r=r):
            j = i - 1
            prev = [(h, j, functools.partial(near_scores, h, j, bias_ref[h, 1] + sel_ref[h, pl.ds(j, 1), :]))
                    for h in heads]
            store_state(tile_pass(own_tiles() + prev + far_tiles(0, r), {}))

    def far_group(it, flat):
        state = tile_pass(far_tiles(rem + MOBA_FAR_GROUP * it, MOBA_FAR_GROUP), {h: flat[h] for h in heads})
        return tuple(state[h] for h in heads)

    state0 = load_state()
    final = lax.fori_loop(0, n_far // MOBA_FAR_GROUP, far_group, tuple(state0[h] for h in heads))

    for h in heads:
        acc = final[h][1]
        o_ref[0, 0, h * HEAD_DIM:(h + 1) * HEAD_DIM, :] = (
            acc[:HEAD_DIM] / acc[HEAD_DIM:HEAD_DIM + 1]).astype(o_ref.dtype)


def _moba(rel_bias, qt, k, vt, kmean, bias_tiles):
    bsz, nb, _, tq = qt.shape
    t = k.shape[1]
    assert nb + MOBA_BIAS_PARTS <= MOBA_EXTRA_ROWS
    esel = jnp.asarray(_moba_selector(nb), BF16)
    once = lambda a: pl.BlockSpec(a.shape, lambda b, i: (0,) * a.ndim, pipeline_mode=pl.Buffered(1))
    return pl.pallas_call(
        _moba_kernel,
        grid=(bsz, nb),
        in_specs=[
            pl.BlockSpec(memory_space=pltpu.SMEM),
            pl.BlockSpec((1, 1, D_ATTN, tq), lambda b, i: (b, i, 0, 0)),
            pl.BlockSpec((1, t, D_ATTN), lambda b, i: (b, 0, 0)),
            pl.BlockSpec((1, nb, D_ATTN, tq), lambda b, i: (b, 0, 0, 0)),
            pl.BlockSpec((1, nb, D_ATTN), lambda b, i: (b, 0, 0)),
            once(bias_tiles), once(esel),
        ],
        out_specs=pl.BlockSpec((1, 1, D_ATTN, tq), lambda b, i: (b, i, 0, 0)),
        out_shape=jax.ShapeDtypeStruct((bsz, nb, D_ATTN, tq), BF16),
        scratch_shapes=[pltpu.VMEM((N_HEADS, 4 * HEAD_DIM, tq), BF16), pltpu.VMEM((N_HEADS, nb, tq), F32),
                        pltpu.VMEM((N_HEADS, 1, tq), F32),
                        pltpu.VMEM((N_HEADS, HEAD_DIM + MOBA_ONES_ROWS, tq), F32)],
        compiler_params=_cparams(2),
        name="moba",
    )(rel_bias, qt, k, vt, kmean, bias_tiles, esel)


def _memkv_kernel(mem_ref, g_ref, wkv_ref, kg_ref, k_ref, v_ref):
    mn = _rms_rows(mem_ref[0], g_ref[...]).astype(BF16)
    kv = jnp.dot(mn, wkv_ref[...], preferred_element_type=F32)
    ks = []
    for hh in range(X_HEADS):
        cs = slice(hh * X_HEAD_DIM, (hh + 1) * X_HEAD_DIM)
        ks.append(_rms_rows(kv[:, cs], kg_ref[...]))
    k_ref[0] = jnp.concatenate(ks, axis=1).astype(BF16)
    v_ref[0] = kv[:, D_X:].astype(BF16)


def _memkv(mem, g, w_kv, kg):
    bsz, m, d = mem.shape
    full = lambda a: pl.BlockSpec(a.shape, lambda b: (0,) * a.ndim)
    return pl.pallas_call(
        _memkv_kernel,
        grid=(bsz,),
        in_specs=[pl.BlockSpec((1, m, d), lambda b: (b, 0, 0)), full(g), full(w_kv), full(kg)],
        out_specs=[pl.BlockSpec((1, m, D_X), lambda b: (b, 0, 0))] * 2,
        out_shape=[jax.ShapeDtypeStruct((bsz, m, D_X), BF16)] * 2,
        compiler_params=_cparams(1),
        name="memkv",
    )(mem, g, w_kv, kg)


OUTX_PIECE = 512


def _outx_kernel(x_ref, ys_ref, yat_ref, ga_ref, wout_ref, gx_ref, wq_ref, qg_ref, kx_ref, vx_ref,
                 wo_ref, o_ref):
    per_piece = OUTX_PIECE // MOBA_BLOCK
    pieces = range(yat_ref.shape[1] // per_piece)
    rows = [slice(r * OUTX_PIECE, (r + 1) * OUTX_PIECE) for r in pieces]
    ya_n = []
    for r in pieces:
        parts = []
        for blk in range(r * per_piece, (r + 1) * per_piece):
            ya = yat_ref[0, blk].astype(F32)
            ms = jnp.mean(ya * ya, axis=0, keepdims=True)
            yan = ya * lax.rsqrt(ms + EPS) * jnp.tile(ga_ref[...], (1, ya.shape[1] // LANES))
            parts.append(yan.T)
        ya_n.append(jnp.concatenate(parts, axis=0).astype(BF16))
    h1 = [x_ref[0, rows[r]]
          + jnp.dot(ys_ref[0, rows[r]], wout_ref[:D_SSM], preferred_element_type=F32)
          + jnp.dot(ya_n[r], wout_ref[D_SSM:], preferred_element_type=F32) for r in pieces]
    hn = [_rms_rows(h1[r], gx_ref[...]).astype(BF16) for r in pieces]
    q = [jnp.dot(hn[r], wq_ref[...], preferred_element_type=F32) for r in pieces]
    outs = [[] for _ in pieces]
    head_cols = [slice(hh * X_HEAD_DIM, (hh + 1) * X_HEAD_DIM) for hh in range(X_HEADS)]

    def scores(hh):
        cs = head_cols[hh]
        qh = [_rms_rows(q[r][:, cs], qg_ref[...]).astype(BF16) for r in pieces]
        return [lax.dot_general(qh[r], kx_ref[0, :, cs], (((1,), (1,)), ((), ())),
                                preferred_element_type=F32) * (X_HEAD_DIM ** -0.5) for r in pieces]

    s_next = scores(0)
    for hh in range(X_HEADS):
        s = s_next
        if hh + 1 < X_HEADS:
            s_next = scores(hh + 1)
        for r in pieces:
            e = jnp.exp(s[r] - jnp.max(s[r], axis=-1, keepdims=True))
            prob = e / jnp.sum(e, axis=-1, keepdims=True)
            outs[r].append(jnp.dot(prob.astype(BF16), vx_ref[0, :, head_cols[hh]],
                                   preferred_element_type=F32))
    for r in pieces:
        o = jnp.concatenate(outs[r], axis=1).astype(BF16)
        o_ref[0, rows[r]] = h1[r] + jnp.dot(o, wo_ref[...], preferred_element_type=F32)


def _outx(x, ys, yat, ga, w_out, gx, w_q, qg, kx, vx, w_o, tm):
    bsz, t, d = x.shape
    m = kx.shape[1]
    rb = tm // MOBA_BLOCK
    full = lambda a: pl.BlockSpec(a.shape, lambda b, i: (0,) * a.ndim)
    return pl.pallas_call(
        _outx_kernel,
        grid=(bsz, t // tm),
        in_specs=[
            pl.BlockSpec((1, tm, d), lambda b, i: (b, i, 0)),
            pl.BlockSpec((1, tm, D_SSM), lambda b, i: (b, i, 0)),
            pl.BlockSpec((1, rb, D_ATTN, MOBA_BLOCK), lambda b, i: (b, i, 0, 0)),
            full(ga), full(w_out), full(gx), full(w_q), full(qg),
            pl.BlockSpec((1, m, D_X), lambda b, i: (b, 0, 0)),
            pl.BlockSpec((1, m, D_X), lambda b, i: (b, 0, 0)),
            full(w_o),
        ],
        out_specs=pl.BlockSpec((1, tm, d), lambda b, i: (b, i, 0)),
        out_shape=jax.ShapeDtypeStruct((bsz, t, d), F32),
        compiler_params=_cparams(2),
        name="outx",
    )(x, ys, yat, ga, w_out, gx, w_q, qg, kx, vx, w_o)


FFN_CHUNK = 1408


def _ffn_kernel(h_ref, g_ref, wup_ref, cw_ref, cb_ref, wdn_ref, o_ref, tail_ref):
    tm = h_ref.shape[1]

    @pl.when(pl.program_id(1) == 0)
    def _():
        tail_ref[...] = jnp.zeros_like(tail_ref)

    h = h_ref[0]
    hn = _rms_rows(h, g_ref[...]).astype(BF16)
    acc = h

    def conv(up, cols):
        ext = jnp.concatenate([tail_ref[:, cols], up], axis=0)
        tail_ref[:, cols] = up[tm - SUBLANES:]
        w = cw_ref[:, cols]
        return (w[0:1] * ext[SUBLANES - 2:SUBLANES - 2 + tm]
                + w[1:2] * ext[SUBLANES - 1:SUBLANES - 1 + tm]
                + w[2:3] * up + cb_ref[:, cols])

    n_chunks = D_FF // FFN_CHUNK
    cols = [(slice(c * FFN_CHUNK, (c + 1) * FFN_CHUNK),
             slice(D_FF + c * FFN_CHUNK, D_FF + (c + 1) * FFN_CHUNK)) for c in range(n_chunks)]
    ups = [tuple(jnp.dot(hn, wup_ref[:, cs], preferred_element_type=F32) for cs in cols[c])
           for c in range(n_chunks)]
    def silu(x):
        half = 0.5 * x
        return half + half * jnp.tanh(half)

    for c in range(n_chunks):
        act = (silu(conv(ups[c][0], cols[c][0])) * conv(ups[c][1], cols[c][1])).astype(BF16)
        acc = acc + jnp.dot(act, wdn_ref[cols[c][0]], preferred_element_type=F32)
    o_ref[0] = acc


def _ffn(h, g, w_up, conv_w, conv_b, w_down, tm):
    bsz, t, d = h.shape
    full = lambda a: pl.BlockSpec(a.shape, lambda b, i: (0,) * a.ndim, pipeline_mode=pl.Buffered(1))
    return pl.pallas_call(
        _ffn_kernel,
        grid=(bsz, t // tm),
        in_specs=[pl.BlockSpec((1, tm, d), lambda b, i: (b, i, 0)),
                  full(g), full(w_up), full(conv_w), full(conv_b), full(w_down)],
        out_specs=pl.BlockSpec((1, tm, d), lambda b, i: (b, i, 0)),
        out_shape=jax.ShapeDtypeStruct((bsz, t, d), F32),
        scratch_shapes=[pltpu.VMEM((SUBLANES, 2 * D_FF), F32)],
        compiler_params=_cparams(2),
        name="ffn",
    )(h, g, w_up, conv_w, conv_b, w_down)


def _block_diag_b(bb):
    g, c, n = bb.shape
    per_tile = S5_NT // n
    per_slab = LANES // c
    tiles = S5_COLS // S5_NT
    place = np.zeros((tiles, per_slab, per_tile), np.float32)
    for j in range(tiles):
        for q in range(per_tile):
            place[j, (j * per_tile + q) % per_slab, q] = 1.0
    out = jnp.einsum('jqcn,jgq->jgcqn', bb.reshape(tiles, per_tile, c, n), jnp.asarray(place))
    return out.reshape(tiles, LANES, S5_NT).astype(BF16)


def _block_diag_c(cm):
    g, c, n = cm.shape
    per_tile = g // 2
    out = jnp.einsum('tqcn,qp->tqnpc', cm.reshape(2, per_tile, c, n), jnp.eye(per_tile, dtype=F32))
    return out.reshape(2, per_tile * n, per_tile * c).astype(BF16)


def kernel(x, mem, norm_mix, w_in, attn_qn, attn_kn, ssm_lam_re, ssm_lam_im, ssm_log_step, ssm_b_re, ssm_b_im, ssm_c_re, ssm_c_im, ssm_d, ssm_w_glu, ssm_b_glu, gnorm_ssm, gnorm_attn, w_out, rel_bias, norm_xattn, norm_mem, x_wq, x_wkv, x_wo, x_qn, x_kn, norm_ffn, ffn_w_up, ffn_conv_w, ffn_conv_b, ffn_w_down):
    bsz, t, d = x.shape
    assert bsz == SUBLANES and t % (2 * max(INPROJ_PIECE, OUTX_PIECE)) == 0
    assert norm_mix.shape[0] == 1
    l = 0
    row = lambda v: v.reshape(1, -1).astype(F32)

    w = w_in[l]
    w_u = w[:, :D_SSM].astype(BF16)
    w_kqvt = jnp.concatenate([w[:, D_SSM + D_ATTN:D_SSM + 2 * D_ATTN], w[:, D_SSM:D_SSM + D_ATTN],
                              w[:, D_SSM + 2 * D_ATTN:]], axis=1).T.astype(BF16)
    lane_rep = lambda v: jnp.broadcast_to(v[:, None], (v.shape[0], LANES)).astype(F32)
    qg = lane_rep(jnp.tile(attn_qn[l], N_HEADS) * (HEAD_DIM ** -0.5))
    kg = lane_rep(jnp.tile(attn_kn[l], N_HEADS))
    u, k, kmean, qt, vt = _inproj(x, row(norm_mix[l]), w_u, w_kqvt, kg, qg, tm=2 * INPROJ_PIECE)

    a_re, a_im, bb_re, bb_im = _s5prep(ssm_lam_re[l], ssm_lam_im[l], ssm_log_step[l][:, None],
                                       jnp.swapaxes(ssm_b_re[l], 1, 2), jnp.swapaxes(ssm_b_im[l], 1, 2))
    a_re8 = jnp.broadcast_to(a_re.reshape(1, -1), (SUBLANES, S5_COLS))
    a_im8 = jnp.broadcast_to(a_im.reshape(1, -1), (SUBLANES, S5_COLS))
    ys = _s5(u, _block_diag_b(bb_re), _block_diag_b(bb_im), a_re8, a_im8,
             _block_diag_c(ssm_c_re[l]), _block_diag_c(ssm_c_im[l]), row(ssm_d[l]),
             ssm_w_glu[l].astype(BF16), row(ssm_b_glu[l]), row(gnorm_ssm[l]), tt=S5_TIME_TILE)

    own, prev = _bucket_tiles()
    bias_tiles = _biastile(rel_bias, jnp.asarray(own), jnp.asarray(prev))
    nb = t // MOBA_BLOCK
    yat = _moba(rel_bias, qt, k, vt, kmean.reshape(bsz, nb, D_ATTN).astype(BF16), bias_tiles)

    kx, vx = _memkv(mem, row(norm_mem[l]), x_wkv[l].astype(BF16), row(x_kn[l]))
    ga = jnp.broadcast_to(gnorm_attn[l][:, None], (D_ATTN, LANES)).astype(F32)
    h2 = _outx(x, ys, yat, ga, w_out[l].astype(BF16), row(norm_xattn[l]), x_wq[l].astype(BF16),
               row(x_qn[l]), kx, vx, x_wo[l].astype(BF16), tm=2 * OUTX_PIECE)

    return _ffn(h2, row(norm_ffn[l]), ffn_w_up[l].astype(BF16), ffn_conv_w[l].reshape(3, 2 * D_FF),
                row(ffn_conv_b[l]), ffn_w_down[l].astype(BF16), tm=FFN_ROWS)
```
